```python
import jax, jax.numpy as jnp
from jax import lax
import numpy as np

D_MODEL = 1024
BATCH = 8
SEQ = 4096
DEPTH = 4

N_MIXERS = 3
N_RET_LAYERS = (DEPTH + 2) // 3
N_CONV_LAYERS = (DEPTH + 1) // 3
N_ATTN_LAYERS = DEPTH // 3

RET_QK_DIM = 256
RET_HEADS = D_MODEL // RET_QK_DIM
RET_V_DIM = 2 * RET_QK_DIM
RET_CHUNK = 128
RET_ROPE_BASE = 10000.0
RET_QK_WIDTH = RET_HEADS * RET_QK_DIM
RET_V_WIDTH = RET_HEADS * RET_V_DIM
RET_IN_WIDTH = 2 * RET_QK_WIDTH + 2 * RET_V_WIDTH

CONV_WIDTH = 31
CONV_PAD = (CONV_WIDTH - 1) // 2

ATTN_HEAD_DIM = 64
ATTN_HEADS = D_MODEL // ATTN_HEAD_DIM
DILATION_GROUPS = ((128, 1), (512, 4), (2048, 16))
N_ATTN_GROUPS = len(DILATION_GROUPS)
ATTN_IN_WIDTH = N_ATTN_GROUPS * 3 * ATTN_HEADS * ATTN_HEAD_DIM
ROPE_THETA = 500000.0
ROT_DIM = ATTN_HEAD_DIM // 4

FFN_HIDDEN = -(-8 * D_MODEL // (3 * 256)) * 256

RMS_EPS = 1e-6
LN_EPS = 1e-5
MASK_VALUE = -1e30

kernel_name = "hybrid_retention_conformer_dilated_encoder"


def _rmsnorm(x, g):
    xf = x.astype(jnp.float32)
    y = xf * lax.rsqrt(jnp.mean(xf * xf, axis=-1, keepdims=True) + RMS_EPS)
    return (y * g.astype(jnp.float32)).astype(x.dtype)


def _layernorm(x, g, b):
    xf = x.astype(jnp.float32)
    mu = jnp.mean(xf, axis=-1, keepdims=True)
    var = jnp.mean(jnp.square(xf - mu), axis=-1, keepdims=True)
    y = (xf - mu) * lax.rsqrt(var + LN_EPS)
    return (y * g.astype(jnp.float32) + b.astype(jnp.float32)).astype(x.dtype)


def _rotate(x, cos, sin):
    half = x.shape[-1] // 2
    x1, x2 = x[..., :half], x[..., half:]
    return jnp.concatenate([x1 * cos - x2 * sin, x2 * cos + x1 * sin], axis=-1).astype(x.dtype)


def _retention_dir(q, k, v, log1m_gamma, strict):
    B, H, S, dk = q.shape
    dv = v.shape[-1]
    C = RET_CHUNK
    n = S // C
    dt = v.dtype
    gamma = 1.0 - jnp.exp(log1m_gamma.astype(jnp.float32))
    lg = jnp.log(gamma)
    idx = jnp.arange(C)
    diff = idx[:, None] - idx[None, :]
    live = (diff > 0) if strict else (diff >= 0)
    dmat = jnp.where(live, jnp.exp(lg[:, None, None] * jnp.where(live, diff, 0)), 0.0)
    xi = jnp.exp(lg[:, None] * (idx + 1)[None, :]).astype(dt)
    zeta = jnp.exp(lg[:, None] * (C - 1 - idx)[None, :]).astype(dt)
    chunk_decay = jnp.exp(lg * C).astype(dt)

    qc = q.reshape(B, H, n, C, dk)
    kc = k.reshape(B, H, n, C, dk)
    vc = v.reshape(B, H, n, C, dv)
    scores = jnp.einsum('bhnid,bhnjd->bhnij', qc, kc) * dmat[None, :, None].astype(dt)
    intra = jnp.einsum('bhnij,bhnje->bhnie', scores, vc)

    def step(state, inp):
        qb, kb, vb = inp
        cross = jnp.einsum('bhcd,bhde->bhce', qb * xi[None, :, :, None], state)
        state = state * chunk_decay[None, :, None, None] + jnp.einsum(
            'bhcd,bhce->bhde', kb * zeta[None, :, :, None], vb)
        return state, cross

    state0 = jnp.zeros((B, H, dk, dv), dt)
    _, cross = lax.scan(step, state0, (jnp.moveaxis(qc, 2, 0), jnp.moveaxis(kc, 2, 0), jnp.moveaxis(vc, 2, 0)))
    out = intra + jnp.moveaxis(cross, 0, 2)
    return out.reshape(B, H, S, dv)


def _retention_mixer(h, w_in, log1m_decay, w_out):
    B, S, _ = h.shape
    proj = h @ w_in
    q = proj[..., :RET_QK_WIDTH]
    k = proj[..., RET_QK_WIDTH:2 * RET_QK_WIDTH]
    v = proj[..., 2 * RET_QK_WIDTH:2 * RET_QK_WIDTH + RET_V_WIDTH]
    g = proj[..., 2 * RET_QK_WIDTH + RET_V_WIDTH:]
    q = q.reshape(B, S, RET_HEADS, RET_QK_DIM).transpose(0, 2, 1, 3)
    k = k.reshape(B, S, RET_HEADS, RET_QK_DIM).transpose(0, 2, 1, 3)
    v = v.reshape(B, S, RET_HEADS, RET_V_DIM).transpose(0, 2, 1, 3)
    inv = 1.0 / (RET_ROPE_BASE ** jnp.linspace(0.0, 1.0, RET_QK_DIM // 2, dtype=jnp.float32))
    ang = jnp.arange(S, dtype=jnp.float32)[:, None] * inv[None, :]
    cos, sin = jnp.cos(ang), jnp.sin(ang)
    q = _rotate(q, cos, sin)
    k = (_rotate(k, cos, sin) * (RET_QK_DIM ** -0.5)).astype(h.dtype)
    fwd = _retention_dir(q, k, v, log1m_decay[0], strict=False)
    bwd = _retention_dir(q[:, :, ::-1], k[:, :, ::-1], v[:, :, ::-1], log1m_decay[1], strict=True)[:, :, ::-1]
    o = (fwd + bwd).astype(jnp.float32)
    o = o * lax.rsqrt(jnp.mean(o * o, axis=-1, keepdims=True) + RMS_EPS)
    o = o.transpose(0, 2, 1, 3).reshape(B, S, RET_V_WIDTH).astype(h.dtype)
    return (jax.nn.silu(g) * o) @ w_out


def _conv_mixer(h, w_in, b_in, w_dw, b_dw, ln_g, ln_b, w_out, b_out):
    a = h @ w_in + b_in
    u = a[..., :D_MODEL] * jax.nn.sigmoid(a[..., D_MODEL:])
    u = lax.conv_general_dilated(
        u, w_dw[:, None, :].astype(u.dtype), window_strides=(1,), padding=[(CONV_PAD, CONV_PAD)],
        dimension_numbers=('NWC', 'WIO', 'NWC'), feature_group_count=D_MODEL) + b_dw
    u = jax.nn.silu(_layernorm(u, ln_g, ln_b))
    return u @ w_out + b_out


def _dilated_band_attention(q, k, v, dil, radius):
    B, S, H, Dh = q.shape
    L = S // dil
    W = radius
    nb = -(-L // W)
    Lp = nb * W

    def strided(t):
        return t.reshape(B, L, dil, H, Dh).transpose(0, 2, 3, 1, 4)

    qs, ks, vs = strided(q), strided(k), strided(v)
    qs = jnp.pad(qs, ((0, 0), (0, 0), (0, 0), (0, Lp - L), (0, 0)))
    kvpad = ((0, 0), (0, 0), (0, 0), (W, Lp - L + W), (0, 0))
    kb = jnp.pad(ks, kvpad).reshape(B, dil, H, nb + 2, W, Dh)
    vb = jnp.pad(vs, kvpad).reshape(B, dil, H, nb + 2, W, Dh)
    qb = qs.reshape(B, dil, H, nb, W, Dh)
    kwin = jnp.concatenate([kb[:, :, :, 0:nb], kb[:, :, :, 1:nb + 1], kb[:, :, :, 2:nb + 2]], axis=4)
    vwin = jnp.concatenate([vb[:, :, :, 0:nb], vb[:, :, :, 1:nb + 1], vb[:, :, :, 2:nb + 2]], axis=4)
    blk = jnp.arange(nb)[:, None]
    qpos = blk * W + jnp.arange(W)[None, :]
    kpos = (blk - 1) * W + jnp.arange(3 * W)[None, :]
    mask = ((jnp.abs(qpos[:, :, None] - kpos[:, None, :]) <= radius)
            & (kpos[:, None, :] >= 0) & (kpos[:, None, :] < L))
    s = jnp.einsum('bghnqd,bghnkd->bghnqk', qb, kwin).astype(jnp.float32) * (Dh ** -0.5)
    s = jnp.where(mask, s, MASK_VALUE)
    mx = jnp.max(s, axis=-1, keepdims=True)
    p = jnp.exp(s - mx)
    den = jnp.sum(p, axis=-1)
    o = jnp.einsum('bghnqk,bghnkd->bghnqd', p.astype(v.dtype), vwin).astype(jnp.float32) / den[..., None]
    lse = mx[..., 0] + jnp.log(den)
    o = o.reshape(B, dil, H, Lp, Dh)[:, :, :, :L].transpose(0, 3, 1, 2, 4).reshape(B, S, H, Dh)
    lse = lse.reshape(B, dil, H, Lp)[:, :, :, :L].transpose(0, 3, 1, 2).reshape(B, S, H)
    return o, lse


def _dilated_attention_mixer(h, w_in, w_out):
    B, S, _ = h.shape
    proj = (h @ w_in).reshape(B, S, N_ATTN_GROUPS, 3, ATTN_HEADS, ATTN_HEAD_DIM)
    inv = ROPE_THETA ** (-jnp.arange(0, ROT_DIM, 2, dtype=jnp.float32) / ROT_DIM)
    ang = jnp.arange(S, dtype=jnp.float32)[:, None] * inv[None, :]
    cos, sin = jnp.cos(ang)[:, None, :], jnp.sin(ang)[:, None, :]

    def prope(t):
        return jnp.concatenate([_rotate(t[..., :ROT_DIM], cos, sin), t[..., ROT_DIM:]], axis=-1)

    outs, lses = [], []
    for g, (window, dil) in enumerate(DILATION_GROUPS):
        q = prope(proj[:, :, g, 0])
        k = prope(proj[:, :, g, 1])
        v = proj[:, :, g, 2]
        o, l = _dilated_band_attention(q, k, v, dil, window // (2 * dil))
        outs.append(o)
        lses.append(l)
    wts = jax.nn.softmax(jnp.stack(lses, 0), axis=0)
    o = jnp.sum(wts[..., None] * jnp.stack(outs, 0), axis=0).astype(h.dtype)
    return o.reshape(B, S, ATTN_HEADS * ATTN_HEAD_DIM) @ w_out


def _swiglu(h, w_in, w_out):
    a = h @ w_in
    return (jax.nn.silu(a[..., :FFN_HIDDEN]) * a[..., FFN_HIDDEN:]) @ w_out


def setup_inputs(seed: int = 0) -> dict:
    key = jax.random.key(seed)
    ks = jax.random.split(key, 20)
    f32 = jnp.float32
    nrm = lambda k, shape, scale: jax.random.normal(k, shape, f32) * scale
    base_decay = -(5.0 + jnp.arange(RET_HEADS, dtype=f32)) * np.float32(np.log(2.0))
    return {
        "x": nrm(ks[0], (BATCH, SEQ, D_MODEL), 1.0),
        "norm_w": 1.0 + nrm(ks[1], (DEPTH, 4, D_MODEL), 0.05),
        "ffn_w_in": nrm(ks[2], (DEPTH, D_MODEL, 2 * FFN_HIDDEN), D_MODEL ** -0.5),
        "ffn_w_out": nrm(ks[3], (DEPTH, FFN_HIDDEN, D_MODEL), FFN_HIDDEN ** -0.5),
        "ret_w_in": nrm(ks[4], (N_RET_LAYERS, D_MODEL, RET_IN_WIDTH), D_MODEL ** -0.5),
        "ret_log1m_decay": base_decay[None, None, :] + nrm(ks[5], (N_RET_LAYERS, 2, RET_HEADS), 0.1),
        "ret_w_out": nrm(ks[6], (N_RET_LAYERS, RET_V_WIDTH, D_MODEL), RET_V_WIDTH ** -0.5),
        "conv_w_in": nrm(ks[7], (N_CONV_LAYERS, D_MODEL, 2 * D_MODEL), D_MODEL ** -0.5),
        "conv_b_in": nrm(ks[8], (N_CONV_LAYERS, 2 * D_MODEL), 0.02),
        "conv_w_dw": nrm(ks[9], (N_CONV_LAYERS, CONV_WIDTH, D_MODEL), CONV_WIDTH ** -0.5),
        "conv_b_dw": nrm(ks[10], (N_CONV_LAYERS, D_MODEL), 0.02),
        "conv_ln_g": 1.0 + nrm(ks[11], (N_CONV_LAYERS, D_MODEL), 0.05),
        "conv_ln_b": nrm(ks[12], (N_CONV_LAYERS, D_MODEL), 0.02),
        "conv_w_out": nrm(ks[13], (N_CONV_LAYERS, D_MODEL, D_MODEL), D_MODEL ** -0.5),
        "conv_b_out": nrm(ks[14], (N_CONV_LAYERS, D_MODEL), 0.02),
        "attn_w_in": nrm(ks[15], (N_ATTN_LAYERS, D_MODEL, ATTN_IN_WIDTH), D_MODEL ** -0.5),
        "attn_w_out": nrm(ks[16], (N_ATTN_LAYERS, ATTN_HEADS * ATTN_HEAD_DIM, D_MODEL),
                          (ATTN_HEADS * ATTN_HEAD_DIM) ** -0.5),
    }


def reference(x, norm_w, ffn_w_in, ffn_w_out, ret_w_in, ret_log1m_decay, ret_w_out,
              conv_w_in, conv_b_in, conv_w_dw, conv_b_dw, conv_ln_g, conv_ln_b,
              conv_w_out, conv_b_out, attn_w_in, attn_w_out):
    for i in range(DEPTH):
        kind, j = i % N_MIXERS, i // N_MIXERS
        hn = _rmsnorm(x, norm_w[i, 0])
        if kind == 0:
            m = _retention_mixer(hn, ret_w_in[j], ret_log1m_decay[j], ret_w_out[j])
        elif kind == 1:
            m = _conv_mixer(hn, conv_w_in[j], conv_b_in[j], conv_w_dw[j], conv_b_dw[j],
                            conv_ln_g[j], conv_ln_b[j], conv_w_out[j], conv_b_out[j])
        else:
            m = _dilated_attention_mixer(hn, attn_w_in[j], attn_w_out[j])
        x = x + _rmsnorm(m, norm_w[i, 1])
        f = _swiglu(_rmsnorm(x, norm_w[i, 2]), ffn_w_in[i], ffn_w_out[i])
        x = x + _rmsnorm(f, norm_w[i, 3])
    return x
```

```python
import functools

import jax
import jax.numpy as jnp
from jax import lax
from jax.experimental import pallas as pl
from jax.experimental.pallas import tpu as pltpu

F32 = jnp.float32
BF16 = jnp.bfloat16

RMS_EPS = 1e-6
LN_EPS = 1e-5
MASK_VALUE = -1e30

RET_HEADS = 4
RET_QK_DIM = 256
RET_V_DIM = 512
RET_ROPE_BASE = 10000.0
CONV_WIDTH = 31
CONV_PAD = (CONV_WIDTH - 1) // 2
ATTN_HEADS = 16
ATTN_HEAD_DIM = 64
DILATION_GROUPS = ((128, 1), (512, 4), (2048, 16))
ROPE_THETA = 500000.0
ROT_DIM = ATTN_HEAD_DIM // 4

LANES = 128
V7X_VMEM_LIMIT_BYTES = 56 * 1024 * 1024

TOKEN_TILE = 512
RET_CHUNK = 256
ATTN_Q_BLOCK = 128
CONV_HALO = 16


def _params(*semantics):
    return pltpu.CompilerParams(dimension_semantics=semantics, vmem_limit_bytes=V7X_VMEM_LIMIT_BYTES)


def _resident(shape):
    zeros = (0,) * len(shape)
    return pl.BlockSpec(shape, lambda *_: zeros, pipeline_mode=pl.Buffered(1))


def _rmsnorm(x, g):
    return x * lax.rsqrt(jnp.mean(x * x, axis=-1, keepdims=True) + RMS_EPS) * g


def _silu(x):
    return x * jax.nn.sigmoid(x)


def _dot(a, b):
    return jnp.dot(a, b, preferred_element_type=F32)


def _dot_nt(a, b):
    return lax.dot_general(a, b, (((1,), (1,)), ((), ())), preferred_element_type=F32)


def _dot_tn(a, b):
    return lax.dot_general(a, b, (((0,), (0,)), ((), ())), preferred_element_type=F32)


def _ffn_tile(x, g_pre, w_in_ref, w_out_ref, g_post, hidden_chunk):
    hidden = w_out_ref.shape[0]
    xn = _rmsnorm(x, g_pre).astype(BF16)
    acc = jnp.zeros(x.shape, F32)
    for c in range(hidden // hidden_chunk):
        lo = c * hidden_chunk
        a = _dot(xn, w_in_ref[:, lo:lo + hidden_chunk])
        b = _dot(xn, w_in_ref[:, hidden + lo:hidden + lo + hidden_chunk])
        h = (_silu(a) * b).astype(BF16)
        acc = acc + _dot(h, w_out_ref[lo:lo + hidden_chunk, :])
    return x + _rmsnorm(acc, g_post)


def _ffn_kernel(x_ref, g_pre_ref, w_in_ref, w_out_ref, g_post_ref, o_ref, *, hidden_chunk):
    o_ref[...] = _ffn_tile(x_ref[...], g_pre_ref[...], w_in_ref, w_out_ref, g_post_ref[...], hidden_chunk)


def _ffn_hidden_chunk(hidden):
    for cand in (512, 256, 128):
        if hidden % cand == 0:
            return cand
    raise ValueError(f"hidden width {hidden} is not a multiple of {LANES}")


def _ffn(x2d, g_pre, w_in, w_out, g_post):
    t, d = x2d.shape
    hidden = w_out.shape[0]
    tm = TOKEN_TILE
    return pl.pallas_call(
        functools.partial(_ffn_kernel, hidden_chunk=_ffn_hidden_chunk(hidden)),
        grid=(t // tm,),
        in_specs=[
            pl.BlockSpec((tm, d), lambda i: (i, 0)),
            _resident((1, d)),
            _resident(w_in.shape),
            _resident(w_out.shape),
            _resident((1, d)),
        ],
        out_specs=pl.BlockSpec((tm, d), lambda i: (i, 0)),
        out_shape=jax.ShapeDtypeStruct((t, d), F32),
        compiler_params=_params("parallel"),
        name="ffn",
    )(x2d, g_pre, w_in, w_out, g_post)


def _ret_proj_kernel(x_ref, g_ref, w_ref, cos_ref, sin_ref, o_ref, *, rot_width, head_dim, plain_chunk):
    xn = _rmsnorm(x_ref[...], g_ref[...]).astype(BF16)
    cos = cos_ref[...]
    sin = sin_ref[...]
    half = head_dim // 2
    for h in range(rot_width // head_dim):
        lo = h * head_dim
        a = _dot(xn, w_ref[:, lo:lo + head_dim])
        x1 = a[:, :half]
        x2 = a[:, half:]
        o_ref[:, lo:lo + half] = (x1 * cos - x2 * sin).astype(BF16)
        o_ref[:, lo + half:lo + head_dim] = (x2 * cos + x1 * sin).astype(BF16)
    width = w_ref.shape[1]
    for lo in range(rot_width, width, plain_chunk):
        o_ref[:, lo:lo + plain_chunk] = _dot(xn, w_ref[:, lo:lo + plain_chunk]).astype(BF16)


def _ret_proj(x2d, g, w, cos, sin, seq):
    t, d = x2d.shape
    width = w.shape[1]
    tm = TOKEN_TILE
    blocks_per_seq = seq // tm
    half = RET_QK_DIM // 2
    return pl.pallas_call(
        functools.partial(_ret_proj_kernel, rot_width=2 * RET_HEADS * RET_QK_DIM, head_dim=RET_QK_DIM,
                          plain_chunk=512),
        grid=(t // tm,),
        in_specs=[
            pl.BlockSpec((tm, d), lambda i: (i, 0)),
            _resident((1, d)),
            _resident(w.shape),
            pl.BlockSpec((tm, half), lambda i: (i % blocks_per_seq, 0)),
            pl.BlockSpec((tm, half), lambda i: (i % blocks_per_seq, 0)),
        ],
        out_specs=pl.BlockSpec((tm, width), lambda i: (i, 0)),
        out_shape=jax.ShapeDtypeStruct((t, width), BF16),
        compiler_params=_params("parallel"),
        name="ret_proj",
    )(x2d, g, w, cos, sin)


def _ret_core_kernel(decay_ref, q_ref, k_ref, v_ref, o_ref, acc_ref, st_ref, *, chunk):
    head = pl.program_id(1)
    seq = q_ref.shape[1]
    n_chunks = seq // chunk
    c = chunk

    def log_gamma(direction):
        log1m = jnp.full((c, 1), decay_ref[direction, head], F32)
        return jnp.log(1.0 - jnp.exp(log1m))

    lgf = log_gamma(0)
    lgb = log_gamma(1)
    idx = lax.broadcasted_iota(jnp.int32, (c, 1), 0).astype(F32)
    xi_f = jnp.exp(lgf * (idx + 1.0))
    zeta_f = jnp.exp(lgf * (c - 1.0 - idx))
    decay_f = jnp.exp(lgf * float(c))
    xi_b = jnp.exp(lgb * (c - idx))
    zeta_b = jnp.exp(lgb * idx)
    decay_b = jnp.exp(lgb * float(c))
    diff = (lax.broadcasted_iota(jnp.int32, (c, c), 0) - lax.broadcasted_iota(jnp.int32, (c, c), 1)).astype(F32)
    dmat = jnp.exp(jnp.where(diff >= 0.0, lgf, -lgb) * diff)

    def rows(i):
        return pl.ds(pl.multiple_of(i * c, c), c)

    def state_update(st, decay, kc, zeta, vc):
        kz = (kc.astype(F32) * zeta).astype(BF16)
        return st * decay + _dot_tn(kz, vc)

    st_ref[...] = jnp.zeros(st_ref.shape, F32)

    def fwd(i, carry):
        r = rows(i)
        qc = q_ref[0, r, :]
        kc = k_ref[0, r, :]
        vc = v_ref[0, r, :]
        p = (_dot_nt(qc, kc) * dmat).astype(BF16)
        st = st_ref[...]
        acc_ref[r, :] = _dot(p, vc) + xi_f * _dot(qc, st.astype(BF16))
        st_ref[...] = state_update(st, decay_f, kc, zeta_f, vc)
        return carry

    lax.fori_loop(0, n_chunks, fwd, 0)

    st_ref[...] = jnp.zeros(st_ref.shape, F32)

    def bwd(j, carry):
        i = n_chunks - 1 - j
        r = rows(i)
        qc = q_ref[0, r, :]
        kc = k_ref[0, r, :]
        vc = v_ref[0, r, :]
        st = st_ref[...]
        o = acc_ref[r, :] + xi_b * _dot(qc, st.astype(BF16))
        o = o * lax.rsqrt(jnp.mean(o * o, axis=-1, keepdims=True) + RMS_EPS)
        o_ref[0, r, :] = o.astype(o_ref.dtype)
        st_ref[...] = state_update(st, decay_b, kc, zeta_b, vc)
        return carry

    lax.fori_loop(0, n_chunks, bwd, 0)


def _ret_core(proj3d, log1m_decay):
    b, seq, _ = proj3d.shape
    dk, dv, heads = RET_QK_DIM, RET_V_DIM, RET_HEADS
    k_blk0 = heads
    v_blk0 = 2 * heads * dk // dv
    return pl.pallas_call(
        functools.partial(_ret_core_kernel, chunk=RET_CHUNK),
        grid=(b, heads),
        in_specs=[
            pl.BlockSpec(memory_space=pltpu.SMEM),
            pl.BlockSpec((1, seq, dk), lambda bi, h: (bi, 0, h)),
            pl.BlockSpec((1, seq, dk), lambda bi, h: (bi, 0, k_blk0 + h)),
            pl.BlockSpec((1, seq, dv), lambda bi, h: (bi, 0, v_blk0 + h)),
        ],
        out_specs=pl.BlockSpec((1, seq, dv), lambda bi, h: (bi, 0, h)),
        out_shape=jax.ShapeDtypeStruct((b, seq, heads * dv), BF16),
        scratch_shapes=[pltpu.VMEM((seq, dv), F32), pltpu.VMEM((dk, dv), F32)],
        compiler_params=_params("parallel", "parallel"),
        name="ret_core",
    )(log1m_decay, proj3d, proj3d, proj3d)


def _ret_out_kernel(o_ref, gate_ref, x_ref, w_ref, g_post_ref, out_ref):
    y = (_silu(gate_ref[...].astype(F32)) * o_ref[...].astype(F32)).astype(BF16)
    out_ref[...] = x_ref[...] + _rmsnorm(_dot(y, w_ref[...]), g_post_ref[...])


def _ret_out(o2d, proj2d, x2d, w_out, g_post):
    t, d = x2d.shape
    vw = o2d.shape[1]
    gate_blk = proj2d.shape[1] // vw - 1
    tm = TOKEN_TILE
    return pl.pallas_call(
        _ret_out_kernel,
        grid=(t // tm,),
        in_specs=[
            pl.BlockSpec((tm, vw), lambda i: (i, 0)),
            pl.BlockSpec((tm, vw), lambda i: (i, gate_blk)),
            pl.BlockSpec((tm, d), lambda i: (i, 0)),
            _resident(w_out.shape),
            _resident((1, d)),
        ],
        out_specs=pl.BlockSpec((tm, d), lambda i: (i, 0)),
        out_shape=jax.ShapeDtypeStruct((t, d), F32),
        compiler_params=_params("parallel"),
        name="ret_out",
    )(o2d, proj2d, x2d, w_out, g_post)


def _retention_layer(x, g_pre, g_post, w_in, log1m_decay, w_out):
    b, seq, d = x.shape
    inv = 1.0 / (RET_ROPE_BASE ** jnp.linspace(0.0, 1.0, RET_QK_DIM // 2, dtype=F32))
    ang = jnp.arange(seq, dtype=F32)[:, None] * inv[None, :]
    x2d = x.reshape(b * seq, d)
    proj = _ret_proj(x2d, g_pre, w_in, jnp.cos(ang), jnp.sin(ang), seq)
    o = _ret_core(proj.reshape(b, seq, -1), log1m_decay)
    return _ret_out(o.reshape(b * seq, -1), proj, x2d, w_out, g_post).reshape(b, seq, d)


def _conv_in_kernel(x_ref, g_ref, w_ref, b_ref, u_ref, *, chunk):
    d = u_ref.shape[1]
    xn = _rmsnorm(x_ref[...], g_ref[...]).astype(BF16)
    for lo in range(0, d, chunk):
        a = _dot(xn, w_ref[:, lo:lo + chunk]) + b_ref[:, lo:lo + chunk]
        gate = _dot(xn, w_ref[:, d + lo:d + lo + chunk]) + b_ref[:, d + lo:d + lo + chunk]
        u_ref[:, lo:lo + chunk] = (a * jax.nn.sigmoid(gate)).astype(BF16)


def _conv_in(x2d, g, w_in, b_in):
    t, d = x2d.shape
    tm = TOKEN_TILE
    return pl.pallas_call(
        functools.partial(_conv_in_kernel, chunk=512),
        grid=(t // tm,),
        in_specs=[
            pl.BlockSpec((tm, d), lambda i: (i, 0)),
            _resident((1, d)),
            _resident(w_in.shape),
            _resident(b_in.shape),
        ],
        out_specs=pl.BlockSpec((tm, d), lambda i: (i, 0)),
        out_shape=jax.ShapeDtypeStruct((t, d), BF16),
        compiler_params=_params("parallel"),
        name="conv_in",
    )(x2d, g, w_in, b_in)


def _conv_out_kernel(u_prev_ref, u_ref, u_next_ref, x_ref, w_dw_ref, b_dw_ref, ln_g_ref, ln_b_ref,
                     w_ref, b_out_ref, g_post_ref, out_ref, ext_ref):
    i = pl.program_id(1)
    tm = u_ref.shape[1]
    halo = u_prev_ref.shape[1]
    taps = w_dw_ref.shape[0]
    pad = (taps - 1) // 2
    prev = u_prev_ref[0].astype(F32)
    nxt = u_next_ref[0].astype(F32)
    ext_ref[0:halo, :] = jnp.where(i > 0, prev, 0.0)
    ext_ref[halo:halo + tm, :] = u_ref[0].astype(F32)
    ext_ref[halo + tm:, :] = jnp.where(i < pl.num_programs(1) - 1, nxt, 0.0)
    acc = jnp.zeros((tm, u_ref.shape[2]), F32) + b_dw_ref[...]
    for k in range(taps):
        lo = halo - pad + k
        acc = acc + ext_ref[lo:lo + tm, :] * w_dw_ref[k:k + 1, :]
    mu = jnp.mean(acc, axis=-1, keepdims=True)
    cen = acc - mu
    var = jnp.mean(cen * cen, axis=-1, keepdims=True)
    y = cen * lax.rsqrt(var + LN_EPS) * ln_g_ref[...] + ln_b_ref[...]
    m = _dot(_silu(y).astype(BF16), w_ref[...]) + b_out_ref[...]
    out_ref[0] = x_ref[0] + _rmsnorm(m, g_post_ref[...])


def _conv_out(u, x, w_dw, b_dw, ln_g, ln_b, w_out, b_out, g_post):
    b, seq, d = x.shape
    tm = TOKEN_TILE
    halo = CONV_HALO
    r = tm // halo
    last = seq // halo - 1
    return pl.pallas_call(
        _conv_out_kernel,
        grid=(b, seq // tm),
        in_specs=[
            pl.BlockSpec((1, halo, d), lambda bi, i: (bi, jnp.maximum(i * r - 1, 0), 0)),
            pl.BlockSpec((1, tm, d), lambda bi, i: (bi, i, 0)),
            pl.BlockSpec((1, halo, d), lambda bi, i: (bi, jnp.minimum((i + 1) * r, last), 0)),
            pl.BlockSpec((1, tm, d), lambda bi, i: (bi, i, 0)),
            _resident(w_dw.shape),
            _resident((1, d)),
            _resident((1, d)),
            _resident((1, d)),
            _resident(w_out.shape),
            _resident((1, d)),
            _resident((1, d)),
        ],
        out_specs=pl.BlockSpec((1, tm, d), lambda bi, i: (bi, i, 0)),
        out_shape=jax.ShapeDtypeStruct((b, seq, d), F32),
        scratch_shapes=[pltpu.VMEM((tm + 2 * halo, d), F32)],
        compiler_params=_params("parallel", "parallel"),
        name="conv_out",
    )(u, u, u, x, w_dw, b_dw, ln_g, ln_b, w_out, b_out, g_post)


def _conv_layer(x, g_pre, g_post, w_in, b_in, w_dw, b_dw, ln_g, ln_b, w_out, b_out):
    b, seq, d = x.shape
    u = _conv_in(x.reshape(b * seq, d), g_pre, w_in, b_in).reshape(b, seq, d)
    return _conv_out(u, x, w_dw, b_dw, ln_g, ln_b, w_out, b_out, g_post)


def _attn_proj_kernel(x_ref, g_ref, w_ref, c_ref, s_lo_ref, s_hi_ref, o_ref, xs_ref, xp_ref, *, dil, rot_width,
                      chunk):
    tm = x_ref.shape[0]
    n = tm // dil
    xn = _rmsnorm(x_ref[...], g_ref[...])
    if dil == 1:
        xp = xn.astype(BF16)
    else:
        for cb in range(xs_ref.shape[0]):
            xs_ref[cb] = xn[:, cb * LANES:(cb + 1) * LANES]
        for r in range(dil):
            for cb in range(xs_ref.shape[0]):
                xp_ref[r * n:(r + 1) * n, cb * LANES:(cb + 1) * LANES] = (
                    xs_ref[cb, pl.ds(r, n, stride=dil), :].astype(BF16))
        xp = xp_ref[...]
    cmul = c_ref[...].reshape(tm, LANES)
    s_lo = s_lo_ref[...].reshape(tm, LANES)
    s_hi = s_hi_ref[...].reshape(tm, LANES)
    half = ROT_DIM // 2
    width = w_ref.shape[1]
    for lo in range(0, width, chunk):
        a = _dot(xp, w_ref[:, lo:lo + chunk])
        for sub in range(0, chunk, LANES):
            piece = a[:, sub:sub + LANES]
            if lo < rot_width:
                piece = (piece * cmul + pltpu.roll(piece, LANES - half, 1) * s_lo
                         + pltpu.roll(piece, half, 1) * s_hi)
            piece = piece.astype(BF16)
            for r in range(dil):
                o_ref[0, r, :, lo + sub:lo + sub + LANES] = piece[r * n:(r + 1) * n, :]


def _attn_proj(x2d, g, w, tables, dil, seq):
    t, d = x2d.shape
    width = w.shape[1]
    tm = TOKEN_TILE
    n = tm // dil
    blocks_per_seq = seq // tm
    b = t // seq
    sub_len = seq // dil
    table_spec = pl.BlockSpec((dil, n, LANES), lambda i: (0, i % blocks_per_seq, 0))
    return pl.pallas_call(
        functools.partial(_attn_proj_kernel, dil=dil, rot_width=2 * ATTN_HEADS * ATTN_HEAD_DIM, chunk=512),
        grid=(t // tm,),
        in_specs=[
            pl.BlockSpec((tm, d), lambda i: (i, 0)),
            _resident((1, d)),
            _resident(w.shape),
            table_spec, table_spec, table_spec,
        ],
        out_specs=pl.BlockSpec((1, dil, n, width), lambda i: (i // blocks_per_seq, 0, i % blocks_per_seq, 0)),
        out_shape=jax.ShapeDtypeStruct((b, dil, sub_len, width), BF16),
        scratch_shapes=[pltpu.VMEM((d // LANES, tm, LANES), F32), pltpu.VMEM((tm, d), BF16)],
        compiler_params=_params("parallel"),
        name=f"attn_proj_d{dil}",
    )(x2d, g, w, *tables)


def _band_attn_kernel(q_ref, kp_ref, kc_ref, kn_ref, vp_ref, vc_ref, vn_ref, o_ref, lse_ref, *,
                      heads, head_dim, radius, sub_len):
    i = pl.program_id(1)
    tq = q_ref.shape[1]
    kwin = jnp.concatenate([kp_ref[0, tq - radius:, :], kc_ref[0], kn_ref[0, :radius, :]], axis=0)
    vwin = jnp.concatenate([vp_ref[0, tq - radius:, :], vc_ref[0], vn_ref[0, :radius, :]], axis=0)
    nk = tq + 2 * radius
    row = lax.broadcasted_iota(jnp.int32, (tq, nk), 0)
    col = lax.broadcasted_iota(jnp.int32, (tq, nk), 1)
    kpos = i * tq - radius + col
    off = col - row
    mask = (off >= 0) & (off <= 2 * radius) & (kpos >= 0) & (kpos < sub_len)
    lse_ref[...] = jnp.zeros(lse_ref.shape, F32)
    for h in range(heads):
        lo = h * head_dim
        s = _dot_nt(q_ref[0, :, lo:lo + head_dim], kwin[:, lo:lo + head_dim])
        s = jnp.where(mask, s, MASK_VALUE)
        mx = jnp.max(s, axis=-1, keepdims=True)
        p = jnp.exp(s - mx)
        den = jnp.sum(p, axis=-1, keepdims=True)
        o = _dot(p.astype(BF16), vwin[:, lo:lo + head_dim]) / den
        o_ref[0, :, lo:lo + head_dim] = o.astype(o_ref.dtype)
        lse_ref[0, :, h:h + 1] = mx + jnp.log(den)


def _band_attn(qkv, dil, radius):
    b, _, sub_len, w3 = qkv.shape
    width = w3 // 3
    tq = ATTN_Q_BLOCK
    nblk = sub_len // tq
    assert radius <= tq and sub_len % tq == 0
    flat = qkv.reshape(b * dil, sub_len, w3)

    def spec(col, shift):
        return pl.BlockSpec((1, tq, width), lambda s, i: (s, jnp.clip(i + shift, 0, nblk - 1), col))

    return pl.pallas_call(
        functools.partial(_band_attn_kernel, heads=ATTN_HEADS, head_dim=ATTN_HEAD_DIM, radius=radius,
                          sub_len=sub_len),
        grid=(b * dil, nblk),
        in_specs=[spec(0, 0), spec(1, -1), spec(1, 0), spec(1, 1), spec(2, -1), spec(2, 0), spec(2, 1)],
        out_specs=[
            pl.BlockSpec((1, tq, width), lambda s, i: (s // dil, i, s % dil)),
            pl.BlockSpec((1, tq, LANES), lambda s, i: (s // dil, i, s % dil)),
        ],
        out_shape=[
            jax.ShapeDtypeStruct((b, sub_len, dil * width), BF16),
            jax.ShapeDtypeStruct((b, sub_len, dil * LANES), F32),
        ],
        compiler_params=_params("parallel", "parallel"),
        name=f"band_attn_d{dil}",
    )(flat, flat, flat, flat, flat, flat, flat)


def _attn_out_kernel(*refs, dils, head_dim):
    n_groups = len(dils)
    o_refs = refs[:n_groups]
    lse_refs = refs[n_groups:2 * n_groups]
    x_ref, w_ref, g_post_ref, out_ref = refs[2 * n_groups:2 * n_groups + 4]
    scratch = refs[2 * n_groups + 4:]
    tm, d = x_ref.shape[1], x_ref.shape[2]

    def natural(ref, dil, width, buf):
        if dil == 1:
            return ref[0].astype(F32)
        n = tm // dil
        for r in range(dil):
            for cb in range(width // LANES):
                lo = r * width + cb * LANES
                buf[cb, pl.ds(r, n, stride=dil), :] = ref[0, :, lo:lo + LANES].astype(F32)
        return jnp.concatenate([buf[cb] for cb in range(width // LANES)], axis=1)

    outs, lses = [], []
    k = 0
    for g, dil in enumerate(dils):
        if dil == 1:
            outs.append(natural(o_refs[g], 1, d, None))
            lses.append(natural(lse_refs[g], 1, LANES, None))
        else:
            outs.append(natural(o_refs[g], dil, d, scratch[k]))
            lses.append(natural(lse_refs[g], dil, LANES, scratch[k + 1]))
            k += 2
    mx = functools.reduce(jnp.maximum, lses)
    es = [jnp.exp(l - mx) for l in lses]
    tot = functools.reduce(lambda a, b_: a + b_, es)
    expand = (lax.broadcasted_iota(jnp.int32, (LANES, d), 1) // head_dim
              == lax.broadcasted_iota(jnp.int32, (LANES, d), 0)).astype(BF16)
    mixed = jnp.zeros((tm, d), F32)
    for e, o in zip(es, outs):
        wgt = e / tot
        hi = wgt.astype(BF16)
        lo = (wgt - hi.astype(F32)).astype(BF16)
        mixed = mixed + (_dot(hi, expand) + _dot(lo, expand)) * o
    m = _dot(mixed.astype(BF16), w_ref[...])
    out_ref[0] = x_ref[0] + _rmsnorm(m, g_post_ref[...])


def _attn_out(os_, lses, x, w_out, g_post, dils):
    b, seq, d = x.shape
    tm = TOKEN_TILE
    in_specs = []
    for dil in dils:
        in_specs.append(pl.BlockSpec((1, tm // dil, dil * d), lambda bi, i: (bi, i, 0)))
    for dil in dils:
        in_specs.append(pl.BlockSpec((1, tm // dil, dil * LANES), lambda bi, i: (bi, i, 0)))
    in_specs += [pl.BlockSpec((1, tm, d), lambda bi, i: (bi, i, 0)), _resident(w_out.shape), _resident((1, d))]
    scratch = []
    for dil in dils:
        if dil > 1:
            scratch += [pltpu.VMEM((d // LANES, tm, LANES), F32), pltpu.VMEM((1, tm, LANES), F32)]
    return pl.pallas_call(
        functools.partial(_attn_out_kernel, dils=dils, head_dim=ATTN_HEAD_DIM),
        grid=(b, seq // tm),
        in_specs=in_specs,
        out_specs=pl.BlockSpec((1, tm, d), lambda bi, i: (bi, i, 0)),
        out_shape=jax.ShapeDtypeStruct((b, seq, d), F32),
        scratch_shapes=scratch,
        compiler_params=_params("parallel", "parallel"),
        name="attn_out",
    )(*os_, *lses, x, w_out, g_post)


def _rotary_tables(seq, dil):
    inv = ROPE_THETA ** (-jnp.arange(0, ROT_DIM, 2, dtype=F32) / ROT_DIM)
    ang = jnp.arange(seq, dtype=F32)[:, None] * inv[None, :]
    cos, sin = jnp.cos(ang), jnp.sin(ang)
    half = ROT_DIM // 2
    rest = ATTN_HEAD_DIM - ROT_DIM
    ones = jnp.ones((seq, rest), F32)
    zeros_h = jnp.zeros((seq, half), F32)
    zeros_r = jnp.zeros((seq, rest), F32)
    per_head = (
        jnp.concatenate([cos, cos, ones], axis=1),
        jnp.concatenate([-sin, zeros_h, zeros_r], axis=1),
        jnp.concatenate([zeros_h, sin, zeros_r], axis=1),
    )
    reps = LANES // ATTN_HEAD_DIM
    out = []
    for tbl in per_head:
        tbl = jnp.tile(tbl, (1, reps))
        out.append(tbl.reshape(seq // dil, dil, LANES).transpose(1, 0, 2))
    return out


def _attention_layer(x, g_pre, g_post, w_in, w_out):
    b, seq, d = x.shape
    x2d = x.reshape(b * seq, d)
    group_width = w_in.shape[1] // len(DILATION_GROUPS)
    dils = tuple(dil for _, dil in DILATION_GROUPS)
    os_, lses = [], []
    for g, (window, dil) in enumerate(DILATION_GROUPS):
        w_g = w_in[:, g * group_width:(g + 1) * group_width]
        qkv = _attn_proj(x2d, g_pre, w_g, _rotary_tables(seq, dil), dil, seq)
        o, lse = _band_attn(qkv, dil, window // (2 * dil))
        os_.append(o)
        lses.append(lse)
    return _attn_out(os_, lses, x, w_out, g_post, dils)


def _scaled_bf16(w, lo, hi, scale):
    col = jnp.arange(w.shape[-1])
    return (w * jnp.where((col >= lo) & (col < hi), scale, 1.0).astype(w.dtype)).astype(BF16)


def kernel(x, norm_w, ffn_w_in, ffn_w_out, ret_w_in, ret_log1m_decay, ret_w_out, conv_w_in, conv_b_in, conv_w_dw,
           conv_b_dw, conv_ln_g, conv_ln_b, conv_w_out, conv_b_out, attn_w_in, attn_w_out):
    depth = norm_w.shape[0]
    b, seq, d = x.shape
    n_mixers = 3
    qk_w = RET_HEADS * RET_QK_DIM
    attn_gw = 3 * ATTN_HEADS * ATTN_HEAD_DIM
    ret_w_in_b = _scaled_bf16(ret_w_in, qk_w, 2 * qk_w, RET_QK_DIM ** -0.5)
    attn_col = jnp.arange(attn_w_in.shape[-1]) % attn_gw
    attn_w_in_b = (attn_w_in * jnp.where(attn_col < ATTN_HEADS * ATTN_HEAD_DIM, ATTN_HEAD_DIM ** -0.5, 1.0)
                   .astype(F32)).astype(BF16)
    row = lambda v: v.reshape(1, -1)
    for i in range(depth):
        kind, j = i % n_mixers, i // n_mixers
        g_pre, g_post = row(norm_w[i, 0]), row(norm_w[i, 1])
        if kind == 0:
            x = _retention_layer(x, g_pre, g_post, ret_w_in_b[j], ret_log1m_decay[j], ret_w_out[j].astype(BF16))
        elif kind == 1:
            x = _conv_layer(x, g_pre, g_post, conv_w_in[j].astype(BF16), row(conv_b_in[j]), conv_w_dw[j],
                            row(conv_b_dw[j]), row(conv_ln_g[j]), row(conv_ln_b[j]), conv_w_out[j].astype(BF16),
                            row(conv_b_out[j]))
        else:
            x = _attention_layer(x, g_pre, g_post, attn_w_in_b[j], attn_w_out[j].astype(BF16))
        x = _ffn(x.reshape(b * seq, d), row(norm_w[i, 2]), ffn_w_in[i].astype(BF16), ffn_w_out[i].astype(BF16),
                 row(norm_w[i, 3])).reshape(b, seq, d)
    return x
```

```python
import functools

import jax
import jax.numpy as jnp
from jax import lax
from jax.experimental import pallas as pl
from jax.experimental.pallas import tpu as pltpu

F32 = jnp.float32
BF16 = jnp.bfloat16

RMS_EPS = 1e-6
LN_EPS = 1e-5
MASK_VALUE = -1e30

RET_HEADS = 4
RET_QK_DIM = 256
RET_V_DIM = 512
RET_ROPE_BASE = 10000.0
CONV_WIDTH = 31
CONV_PAD = (CONV_WIDTH - 1) // 2
ATTN_HEADS = 16
ATTN_HEAD_DIM = 64
DILATION_GROUPS = ((128, 1), (512, 4), (2048, 16))
ROPE_THETA = 500000.0
ROT_DIM = ATTN_HEAD_DIM // 4

LANES = 128
V7X_VMEM_LIMIT_BYTES = 56 * 1024 * 1024

TOKEN_TILE = 512
RET_CHUNK = 256
ATTN_Q_BLOCK = 128
CONV_HALO = 16
CONV_STRIP = 128


def _params(*semantics):
    return pltpu.CompilerParams(dimension_semantics=semantics, vmem_limit_bytes=V7X_VMEM_LIMIT_BYTES)


def _resident(shape):
    zeros = (0,) * len(shape)
    return pl.BlockSpec(shape, lambda *_: zeros, pipeline_mode=pl.Buffered(1))


def _rmsnorm(x, g):
    return x * lax.rsqrt(jnp.mean(x * x, axis=-1, keepdims=True) + RMS_EPS) * g


def _silu(x):
    return x * jax.nn.sigmoid(x)


def _dot(a, b):
    return jnp.dot(a, b, preferred_element_type=F32)


def _dot_nt(a, b):
    return lax.dot_general(a, b, (((1,), (1,)), ((), ())), preferred_element_type=F32)


def _dot_tn(a, b):
    return lax.dot_general(a, b, (((0,), (0,)), ((), ())), preferred_element_type=F32)


def _ffn_tile(x, g_pre, w_in_ref, w_out_ref, g_post, hidden_chunk):
    hidden = w_out_ref.shape[0]
    xn = _rmsnorm(x, g_pre).astype(BF16)
    acc = jnp.zeros(x.shape, F32)
    for c in range(hidden // hidden_chunk):
        lo = c * hidden_chunk
        a = _dot(xn, w_in_ref[:, lo:lo + hidden_chunk])
        b = _dot(xn, w_in_ref[:, hidden + lo:hidden + lo + hidden_chunk])
        h = (_silu(a) * b).astype(BF16)
        acc = acc + _dot(h, w_out_ref[lo:lo + hidden_chunk, :])
    return x + _rmsnorm(acc, g_post)


def _ffn_kernel(x_ref, g_pre_ref, w_in_ref, w_out_ref, g_post_ref, o_ref, *, hidden_chunk):
    o_ref[...] = _ffn_tile(x_ref[...], g_pre_ref[...], w_in_ref, w_out_ref, g_post_ref[...], hidden_chunk)


def _ffn_hidden_chunk(hidden):
    for cand in (512, 256, 128):
        if hidden % cand == 0:
            return cand
    raise ValueError(f"hidden width {hidden} is not a multiple of {LANES}")


def _ffn(x2d, g_pre, w_in, w_out, g_post):
    t, d = x2d.shape
    hidden = w_out.shape[0]
    tm = TOKEN_TILE
    return pl.pallas_call(
        functools.partial(_ffn_kernel, hidden_chunk=_ffn_hidden_chunk(hidden)),
        grid=(t // tm,),
        in_specs=[
            pl.BlockSpec((tm, d), lambda i: (i, 0)),
            _resident((1, d)),
            _resident(w_in.shape),
            _resident(w_out.shape),
            _resident((1, d)),
        ],
        out_specs=pl.BlockSpec((tm, d), lambda i: (i, 0)),
        out_shape=jax.ShapeDtypeStruct((t, d), F32),
        compiler_params=_params("parallel"),
        name="ffn",
    )(x2d, g_pre, w_in, w_out, g_post)


def _ret_proj_kernel(x_ref, g_ref, w_ref, cos_ref, sin_ref, o_ref, *, rot_width, head_dim, plain_chunk):
    xn = _rmsnorm(x_ref[...], g_ref[...]).astype(BF16)
    cos = cos_ref[...]
    sin = sin_ref[...]
    half = head_dim // 2
    for h in range(rot_width // head_dim):
        lo = h * head_dim
        a = _dot(xn, w_ref[:, lo:lo + head_dim])
        x1 = a[:, :half]
        x2 = a[:, half:]
        o_ref[:, lo:lo + half] = (x1 * cos - x2 * sin).astype(BF16)
        o_ref[:, lo + half:lo + head_dim] = (x2 * cos + x1 * sin).astype(BF16)
    width = w_ref.shape[1]
    for lo in range(rot_width, width, plain_chunk):
        o_ref[:, lo:lo + plain_chunk] = _dot(xn, w_ref[:, lo:lo + plain_chunk]).astype(BF16)


def _ret_proj(x2d, g, w, cos, sin, seq):
    t, d = x2d.shape
    width = w.shape[1]
    tm = TOKEN_TILE
    blocks_per_seq = seq // tm
    half = RET_QK_DIM // 2
    return pl.pallas_call(
        functools.partial(_ret_proj_kernel, rot_width=2 * RET_HEADS * RET_QK_DIM, head_dim=RET_QK_DIM,
                          plain_chunk=512),
        grid=(t // tm,),
        in_specs=[
            pl.BlockSpec((tm, d), lambda i: (i, 0)),
            _resident((1, d)),
            _resident(w.shape),
            pl.BlockSpec((tm, half), lambda i: (i % blocks_per_seq, 0)),
            pl.BlockSpec((tm, half), lambda i: (i % blocks_per_seq, 0)),
        ],
        out_specs=pl.BlockSpec((tm, width), lambda i: (i, 0)),
        out_shape=jax.ShapeDtypeStruct((t, width), BF16),
        compiler_params=_params("parallel"),
        name="ret_proj",
    )(x2d, g, w, cos, sin)


def _ret_core_kernel(decay_ref, q_ref, k_ref, v_ref, o_ref, acc_ref, stf_ref, stb_ref, *, chunk):
    head = pl.program_id(1)
    seq = q_ref.shape[1]
    n_chunks = seq // chunk
    c = chunk

    def log_gamma(direction):
        log1m = jnp.full((c, 1), decay_ref[direction, head], F32)
        return jnp.log(1.0 - jnp.exp(log1m))

    lgf = log_gamma(0)
    lgb = log_gamma(1)
    idx = lax.broadcasted_iota(jnp.int32, (c, 1), 0).astype(F32)
    xi_f = jnp.exp(lgf * (idx + 1.0))
    zeta_f = jnp.exp(lgf * (c - 1.0 - idx))
    decay_f = jnp.exp(lgf * float(c))
    xi_b = jnp.exp(lgb * (c - idx))
    zeta_b = jnp.exp(lgb * idx)
    decay_b = jnp.exp(lgb * float(c))
    diff = (lax.broadcasted_iota(jnp.int32, (c, c), 0) - lax.broadcasted_iota(jnp.int32, (c, c), 1)).astype(F32)
    dmat = jnp.exp(jnp.where(diff >= 0.0, lgf, -lgb) * diff)

    def rows(i):
        return pl.ds(pl.multiple_of(i * c, c), c)

    def state_update(st, decay, kc, zeta, vc):
        kz = (kc.astype(F32) * zeta).astype(BF16)
        return st * decay + _dot_tn(kz, vc)

    stf_ref[...] = jnp.zeros(stf_ref.shape, F32)
    stb_ref[...] = jnp.zeros(stb_ref.shape, F32)

    def parts(a, b):
        ra, rb = rows(a), rows(b)
        qa, ka, va = q_ref[0, ra, :], k_ref[0, ra, :], v_ref[0, ra, :]
        qb, kb, vb = q_ref[0, rb, :], k_ref[0, rb, :], v_ref[0, rb, :]
        stf = stf_ref[...]
        stb = stb_ref[...]
        scores = _dot_nt(qa, ka)
        cross_f = _dot(qa, stf.astype(BF16))
        cross_b = _dot(qb, stb.astype(BF16))
        stf_ref[...] = state_update(stf, decay_f, ka, zeta_f, va)
        stb_ref[...] = state_update(stb, decay_b, kb, zeta_b, vb)
        intra = _dot((scores * dmat).astype(BF16), va)
        return intra + xi_f * cross_f, xi_b * cross_b

    def finish(i, o):
        o = o * lax.rsqrt(jnp.mean(o * o, axis=-1, keepdims=True) + RMS_EPS)
        o_ref[0, rows(i), :] = o.astype(o_ref.dtype)

    half = n_chunks // 2

    def approach(j, carry):
        a, b = j, n_chunks - 1 - j
        part_a, part_b = parts(a, b)
        acc_ref[rows(a), :] = part_a
        acc_ref[rows(b), :] = part_b
        return carry

    def cross(j, carry):
        a, b = half + j, half - 1 - j
        part_a, part_b = parts(a, b)
        finish(a, acc_ref[rows(a), :] + part_a)
        finish(b, acc_ref[rows(b), :] + part_b)
        return carry

    lax.fori_loop(0, half, approach, 0, unroll=True)
    lax.fori_loop(0, half, cross, 0, unroll=True)


def _ret_core(proj3d, log1m_decay):
    b, seq, _ = proj3d.shape
    dk, dv, heads = RET_QK_DIM, RET_V_DIM, RET_HEADS
    assert seq % (2 * RET_CHUNK) == 0
    k_blk0 = heads
    v_blk0 = 2 * heads * dk // dv
    return pl.pallas_call(
        functools.partial(_ret_core_kernel, chunk=RET_CHUNK),
        grid=(b, heads),
        in_specs=[
            pl.BlockSpec(memory_space=pltpu.SMEM),
            pl.BlockSpec((1, seq, dk), lambda bi, h: (bi, 0, h)),
            pl.BlockSpec((1, seq, dk), lambda bi, h: (bi, 0, k_blk0 + h)),
            pl.BlockSpec((1, seq, dv), lambda bi, h: (bi, 0, v_blk0 + h)),
        ],
        out_specs=pl.BlockSpec((1, seq, dv), lambda bi, h: (bi, 0, h)),
        out_shape=jax.ShapeDtypeStruct((b, seq, heads * dv), BF16),
        scratch_shapes=[pltpu.VMEM((seq, dv), F32), pltpu.VMEM((dk, dv), F32), pltpu.VMEM((dk, dv), F32)],
        compiler_params=_params("parallel", "parallel"),
        name="ret_core",
    )(log1m_decay, proj3d, proj3d, proj3d)


def _ret_out_kernel(o_ref, gate_ref, x_ref, w_ref, g_post_ref, out_ref):
    y = (_silu(gate_ref[...].astype(F32)) * o_ref[...].astype(F32)).astype(BF16)
    out_ref[...] = x_ref[...] + _rmsnorm(_dot(y, w_ref[...]), g_post_ref[...])


def _ret_out(o2d, proj2d, x2d, w_out, g_post):
    t, d = x2d.shape
    vw = o2d.shape[1]
    gate_blk = proj2d.shape[1] // vw - 1
    tm = TOKEN_TILE
    return pl.pallas_call(
        _ret_out_kernel,
        grid=(t // tm,),
        in_specs=[
            pl.BlockSpec((tm, vw), lambda i: (i, 0)),
            pl.BlockSpec((tm, vw), lambda i: (i, gate_blk)),
            pl.BlockSpec((tm, d), lambda i: (i, 0)),
            _resident(w_out.shape),
            _resident((1, d)),
        ],
        out_specs=pl.BlockSpec((tm, d), lambda i: (i, 0)),
        out_shape=jax.ShapeDtypeStruct((t, d), F32),
        compiler_params=_params("parallel"),
        name="ret_out",
    )(o2d, proj2d, x2d, w_out, g_post)


def _retention_layer(x, g_pre, g_post, w_in, log1m_decay, w_out):
    b, seq, d = x.shape
    inv = 1.0 / (RET_ROPE_BASE ** jnp.linspace(0.0, 1.0, RET_QK_DIM // 2, dtype=F32))
    ang = jnp.arange(seq, dtype=F32)[:, None] * inv[None, :]
    x2d = x.reshape(b * seq, d)
    proj = _ret_proj(x2d, g_pre, w_in, jnp.cos(ang), jnp.sin(ang), seq)
    o = _ret_core(proj.reshape(b, seq, -1), log1m_decay)
    return _ret_out(o.reshape(b * seq, -1), proj, x2d, w_out, g_post).reshape(b, seq, d)


def _conv_in_kernel(x_ref, g_ref, w_ref, b_ref, u_ref, *, chunk):
    d = u_ref.shape[1]
    xn = _rmsnorm(x_ref[...], g_ref[...]).astype(BF16)
    for lo in range(0, d, chunk):
        a = _dot(xn, w_ref[:, lo:lo + chunk]) + b_ref[:, lo:lo + chunk]
        gate = _dot(xn, w_ref[:, d + lo:d + lo + chunk]) + b_ref[:, d + lo:d + lo + chunk]
        u_ref[:, lo:lo + chunk] = (a * jax.nn.sigmoid(gate)).astype(BF16)


def _conv_in(x2d, g, w_in, b_in):
    t, d = x2d.shape
    tm = TOKEN_TILE
    return pl.pallas_call(
        functools.partial(_conv_in_kernel, chunk=512),
        grid=(t // tm,),
        in_specs=[
            pl.BlockSpec((tm, d), lambda i: (i, 0)),
            _resident((1, d)),
            _resident(w_in.shape),
            _resident(b_in.shape),
        ],
        out_specs=pl.BlockSpec((tm, d), lambda i: (i, 0)),
        out_shape=jax.ShapeDtypeStruct((t, d), BF16),
        compiler_params=_params("parallel"),
        name="conv_in",
    )(x2d, g, w_in, b_in)


def _conv_out_kernel(u_prev_ref, u_ref, u_next_ref, x_ref, w_dw_ref, b_dw_ref, ln_g_ref, ln_b_ref,
                     w_ref, b_out_ref, g_post_ref, out_ref, ext_ref, *, strip):
    i = pl.program_id(1)
    tm = u_ref.shape[1]
    halo = u_prev_ref.shape[1]
    taps = w_dw_ref.shape[0]
    pad = (taps - 1) // 2
    prev = u_prev_ref[0].astype(F32)
    nxt = u_next_ref[0].astype(F32)
    ext_ref[0:halo, :] = jnp.where(i > 0, prev, 0.0)
    ext_ref[halo:halo + tm, :] = u_ref[0].astype(F32)
    ext_ref[halo + tm:, :] = jnp.where(i < pl.num_programs(1) - 1, nxt, 0.0)
    sub = 8
    rows_w = strip + 2 * halo
    strips = []
    for r0 in range(0, tm, strip):
        window = ext_ref[r0:r0 + rows_w, :]
        acc = jnp.zeros((strip, u_ref.shape[2]), F32) + b_dw_ref[...]
        for res in range(sub):
            shifted = window if res == 0 else pltpu.roll(window, rows_w - res, 0)
            for k in range(taps):
                off = halo - pad + k
                if off % sub == res:
                    acc = acc + shifted[off - res:off - res + strip, :] * w_dw_ref[k:k + 1, :]
        strips.append(acc)
    acc = jnp.concatenate(strips, axis=0)
    mu = jnp.mean(acc, axis=-1, keepdims=True)
    cen = acc - mu
    var = jnp.mean(cen * cen, axis=-1, keepdims=True)
    y = cen * lax.rsqrt(var + LN_EPS) * ln_g_ref[...] + ln_b_ref[...]
    m = _dot(_silu(y).astype(BF16), w_ref[...]) + b_out_ref[...]
    out_ref[0] = x_ref[0] + _rmsnorm(m, g_post_ref[...])


def _conv_out(u, x, w_dw, b_dw, ln_g, ln_b, w_out, b_out, g_post):
    b, seq, d = x.shape
    tm = TOKEN_TILE
    halo = CONV_HALO
    r = tm // halo
    last = seq // halo - 1
    return pl.pallas_call(
        functools.partial(_conv_out_kernel, strip=CONV_STRIP),
        grid=(b, seq // tm),
        in_specs=[
            pl.BlockSpec((1, halo, d), lambda bi, i: (bi, jnp.maximum(i * r - 1, 0), 0)),
            pl.BlockSpec((1, tm, d), lambda bi, i: (bi, i, 0)),
            pl.BlockSpec((1, halo, d), lambda bi, i: (bi, jnp.minimum((i + 1) * r, last), 0)),
            pl.BlockSpec((1, tm, d), lambda bi, i: (bi, i, 0)),
            _resident(w_dw.shape),
            _resident((1, d)),
            _resident((1, d)),
            _resident((1, d)),
            _resident(w_out.shape),
            _resident((1, d)),
            _resident((1, d)),
        ],
        out_specs=pl.BlockSpec((1, tm, d), lambda bi, i: (bi, i, 0)),
        out_shape=jax.ShapeDtypeStruct((b, seq, d), F32),
        scratch_shapes=[pltpu.VMEM((tm + 2 * halo, d), F32)],
        compiler_params=_params("parallel", "parallel"),
        name="conv_out",
    )(u, u, u, x, w_dw, b_dw, ln_g, ln_b, w_out, b_out, g_post)


def _conv_layer(x, g_pre, g_post, w_in, b_in, w_dw, b_dw, ln_g, ln_b, w_out, b_out):
    b, seq, d = x.shape
    u = _conv_in(x.reshape(b * seq, d), g_pre, w_in, b_in).reshape(b, seq, d)
    return _conv_out(u, x, w_dw, b_dw, ln_g, ln_b, w_out, b_out, g_post)


def _attn_proj_kernel(x_ref, g_ref, w_ref, c_ref, s_lo_ref, s_hi_ref, o_ref, xs_ref, xp_ref, *, dil, rot_width,
                      chunk):
    tm = x_ref.shape[0]
    n = tm // dil
    xn = _rmsnorm(x_ref[...], g_ref[...])
    if dil == 1:
        xp = xn.astype(BF16)
    else:
        for cb in range(xs_ref.shape[0]):
            xs_ref[cb] = xn[:, cb * LANES:(cb + 1) * LANES]
        for r in range(dil):
            for cb in range(xs_ref.shape[0]):
                xp_ref[r * n:(r + 1) * n, cb * LANES:(cb + 1) * LANES] = (
                    xs_ref[cb, pl.ds(r, n, stride=dil), :].astype(BF16))
        xp = xp_ref[...]
    cmul = c_ref[...].reshape(tm, LANES)
    s_lo = s_lo_ref[...].reshape(tm, LANES)
    s_hi = s_hi_ref[...].reshape(tm, LANES)
    half = ROT_DIM // 2
    width = w_ref.shape[1]
    for lo in range(0, width, chunk):
        a = _dot(xp, w_ref[:, lo:lo + chunk])
        for sub in range(0, chunk, LANES):
            piece = a[:, sub:sub + LANES]
            if lo < rot_width:
                piece = (piece * cmul + pltpu.roll(piece, LANES - half, 1) * s_lo
                         + pltpu.roll(piece, half, 1) * s_hi)
            piece = piece.astype(BF16)
            for r in range(dil):
                o_ref[0, r, :, lo + sub:lo + sub + LANES] = piece[r * n:(r + 1) * n, :]


def _attn_proj(x2d, g, w, tables, dil, seq):
    t, d = x2d.shape
    width = w.shape[1]
    tm = TOKEN_TILE
    n = tm // dil
    blocks_per_seq = seq // tm
    b = t // seq
    sub_len = seq // dil
    table_spec = pl.BlockSpec((dil, n, LANES), lambda i: (0, i % blocks_per_seq, 0))
    return pl.pallas_call(
        functools.partial(_attn_proj_kernel, dil=dil, rot_width=2 * ATTN_HEADS * ATTN_HEAD_DIM, chunk=512),
        grid=(t // tm,),
        in_specs=[
            pl.BlockSpec((tm, d), lambda i: (i, 0)),
            _resident((1, d)),
            _resident(w.shape),
            table_spec, table_spec, table_spec,
        ],
        out_specs=pl.BlockSpec((1, dil, n, width), lambda i: (i // blocks_per_seq, 0, i % blocks_per_seq, 0)),
        out_shape=jax.ShapeDtypeStruct((b, dil, sub_len, width), BF16),
        scratch_shapes=[pltpu.VMEM((d // LANES, tm, LANES), F32), pltpu.VMEM((tm, d), BF16)],
        compiler_params=_params("parallel"),
        name=f"attn_proj_d{dil}",
    )(x2d, g, w, *tables)


def _band_attn_kernel(q_ref, kp_ref, kc_ref, kn_ref, vp_ref, vc_ref, vn_ref, o_ref, stat_ref, bias_ref, *,
                      heads, head_dim, radius, sub_len):
    i = pl.program_id(1)
    tq = q_ref.shape[1]
    kwin = jnp.concatenate([kp_ref[0, tq - radius:, :], kc_ref[0], kn_ref[0, :radius, :]], axis=0)
    vwin = jnp.concatenate([vp_ref[0, tq - radius:, :], vc_ref[0], vn_ref[0, :radius, :]], axis=0)
    nk = tq + 2 * radius
    row = lax.broadcasted_iota(jnp.int32, (tq, nk), 0)
    col = lax.broadcasted_iota(jnp.int32, (tq, nk), 1)
    kpos = i * tq - radius + col
    off = col - row
    mask = (off >= 0) & (off <= 2 * radius) & (kpos >= 0) & (kpos < sub_len)
    bias_ref[...] = jnp.where(mask, 0.0, MASK_VALUE)
    lane = lax.broadcasted_iota(jnp.int32, (tq, LANES), 1)
    first = lane < head_dim
    keep_first = first.astype(F32).astype(BF16)
    keep_second = (1.0 - first.astype(F32)).astype(BF16)
    ones = jnp.ones((nk, LANES), BF16)
    stat = jnp.zeros((tq, LANES), F32)
    for pair in range(heads // 2):
        cols = slice(pair * LANES, (pair + 1) * LANES)
        qp = q_ref[0, :, cols]
        s2 = _dot_nt(jnp.concatenate([qp * keep_first, qp * keep_second], axis=0), kwin[:, cols])
        outs = []
        for k in range(2):
            h = 2 * pair + k
            s = s2[k * tq:(k + 1) * tq] + bias_ref[...]
            mx = jnp.max(s, axis=-1, keepdims=True)
            p = jnp.exp(s - mx).astype(BF16)
            pv = _dot(p, jnp.concatenate([vwin[:, cols], ones], axis=1))
            outs.append(pv[:, :LANES])
            stat = jnp.where(lane == h, mx, stat)
            stat = jnp.where(lane == heads + h, pv[:, LANES:], stat)
        o_ref[0, :, cols] = jnp.where(first, outs[0], outs[1]).astype(o_ref.dtype)
    stat_ref[0] = stat


def _band_attn(qkv, dil, radius):
    assert 2 * ATTN_HEADS <= LANES and 2 * ATTN_HEAD_DIM == LANES
    b, _, sub_len, w3 = qkv.shape
    width = w3 // 3
    tq = ATTN_Q_BLOCK
    nblk = sub_len // tq
    assert radius <= tq and sub_len % tq == 0
    flat = qkv.reshape(b * dil, sub_len, w3)

    def spec(col, shift):
        return pl.BlockSpec((1, tq, width), lambda s, i: (s, jnp.clip(i + shift, 0, nblk - 1), col))

    return pl.pallas_call(
        functools.partial(_band_attn_kernel, heads=ATTN_HEADS, head_dim=ATTN_HEAD_DIM, radius=radius,
                          sub_len=sub_len),
        grid=(b * dil, nblk),
        in_specs=[spec(0, 0), spec(1, -1), spec(1, 0), spec(1, 1), spec(2, -1), spec(2, 0), spec(2, 1)],
        out_specs=[
            pl.BlockSpec((1, tq, width), lambda s, i: (s // dil, i, s % dil)),
            pl.BlockSpec((1, tq, LANES), lambda s, i: (s // dil, i, s % dil)),
        ],
        out_shape=[
            jax.ShapeDtypeStruct((b, sub_len, dil * width), BF16),
            jax.ShapeDtypeStruct((b, sub_len, dil * LANES), F32),
        ],
        scratch_shapes=[pltpu.VMEM((tq, tq + 2 * radius), F32)],
        compiler_params=_params("parallel", "parallel"),
        name=f"band_attn_d{dil}",
    )(flat, flat, flat, flat, flat, flat, flat)


def _attn_out_kernel(*refs, dils, head_dim):
    n_groups = len(dils)
    o_refs = refs[:n_groups]
    lse_refs = refs[n_groups:2 * n_groups]
    x_ref, w_ref, g_post_ref, out_ref = refs[2 * n_groups:2 * n_groups + 4]
    scratch = refs[2 * n_groups + 4:]
    tm, d = x_ref.shape[1], x_ref.shape[2]

    def natural(ref, dil, width, buf):
        if dil == 1:
            return ref[0].astype(F32)
        n = tm // dil
        for r in range(dil):
            for cb in range(width // LANES):
                lo = r * width + cb * LANES
                buf[cb, pl.ds(r, n, stride=dil), :] = ref[0, :, lo:lo + LANES].astype(F32)
        return jnp.concatenate([buf[cb] for cb in range(width // LANES)], axis=1)

    outs, lses = [], []
    k = 0
    for g, dil in enumerate(dils):
        if dil == 1:
            outs.append(natural(o_refs[g], 1, d, None))
            lses.append(natural(lse_refs[g], 1, LANES, None))
        else:
            outs.append(natural(o_refs[g], dil, d, scratch[k]))
            lses.append(natural(lse_refs[g], dil, LANES, scratch[k + 1]))
            k += 2
    heads = d // head_dim
    head_lane = lax.broadcasted_iota(jnp.int32, (tm, LANES), 1) < heads
    mx = functools.reduce(jnp.maximum, lses)
    es = [jnp.exp(l - mx) for l in lses]
    dens = [pltpu.roll(l, LANES - heads, 1) for l in lses]
    tot = functools.reduce(lambda a, b_: a + b_, [e * dn for e, dn in zip(es, dens)])
    tot = jnp.where(head_lane, tot, 1.0)
    expand = (lax.broadcasted_iota(jnp.int32, (2 * LANES, d), 1) // head_dim
              == lax.broadcasted_iota(jnp.int32, (2 * LANES, d), 0) % LANES).astype(BF16)
    mixed = jnp.zeros((tm, d), F32)
    for e, o in zip(es, outs):
        wgt = jnp.where(head_lane, e / tot, 0.0)
        hi = wgt.astype(BF16)
        lo = (wgt - hi.astype(F32)).astype(BF16)
        mixed = mixed + _dot(jnp.concatenate([hi, lo], axis=1), expand) * o
    m = _dot(mixed.astype(BF16), w_ref[...])
    out_ref[0] = x_ref[0] + _rmsnorm(m, g_post_ref[...])


def _attn_out(os_, lses, x, w_out, g_post, dils):
    b, seq, d = x.shape
    tm = TOKEN_TILE
    in_specs = []
    for dil in dils:
        in_specs.append(pl.BlockSpec((1, tm // dil, dil * d), lambda bi, i: (bi, i, 0)))
    for dil in dils:
        in_specs.append(pl.BlockSpec((1, tm // dil, dil * LANES), lambda bi, i: (bi, i, 0)))
    in_specs += [pl.BlockSpec((1, tm, d), lambda bi, i: (bi, i, 0)), _resident(w_out.shape), _resident((1, d))]
    scratch = []
    for dil in dils:
        if dil > 1:
            scratch += [pltpu.VMEM((d // LANES, tm, LANES), F32), pltpu.VMEM((1, tm, LANES), F32)]
    return pl.pallas_call(
        functools.partial(_attn_out_kernel, dils=dils, head_dim=ATTN_HEAD_DIM),
        grid=(b, seq // tm),
        in_specs=in_specs,
        out_specs=pl.BlockSpec((1, tm, d), lambda bi, i: (bi, i, 0)),
        out_shape=jax.ShapeDtypeStruct((b, seq, d), F32),
        scratch_shapes=scratch,
        compiler_params=_params("parallel", "parallel"),
        name="attn_out",
    )(*os_, *lses, x, w_out, g_post)


def _rotary_tables(seq, dil):
    inv = ROPE_THETA ** (-jnp.arange(0, ROT_DIM, 2, dtype=F32) / ROT_DIM)
    ang = jnp.arange(seq, dtype=F32)[:, None] * inv[None, :]
    cos, sin = jnp.cos(ang), jnp.sin(ang)
    half = ROT_DIM // 2
    rest = ATTN_HEAD_DIM - ROT_DIM
    ones = jnp.ones((seq, rest), F32)
    zeros_h = jnp.zeros((seq, half), F32)
    zeros_r = jnp.zeros((seq, rest), F32)
    per_head = (
        jnp.concatenate([cos, cos, ones], axis=1),
        jnp.concatenate([-sin, zeros_h, zeros_r], axis=1),
        jnp.concatenate([zeros_h, sin, zeros_r], axis=1),
    )
    reps = LANES // ATTN_HEAD_DIM
    out = []
    for tbl in per_head:
        tbl = jnp.tile(tbl, (1, reps))
        out.append(tbl.reshape(seq // dil, dil, LANES).transpose(1, 0, 2))
    return out


def _attention_layer(x, g_pre, g_post, w_in, w_out):
    b, seq, d = x.shape
    x2d = x.reshape(b * seq, d)
    group_width = w_in.shape[1] // len(DILATION_GROUPS)
    dils = tuple(dil for _, dil in DILATION_GROUPS)
    os_, lses = [], []
    for g, (window, dil) in enumerate(DILATION_GROUPS):
        w_g = w_in[:, g * group_width:(g + 1) * group_width]
        qkv = _attn_proj(x2d, g_pre, w_g, _rotary_tables(seq, dil), dil, seq)
        o, lse = _band_attn(qkv, dil, window // (2 * dil))
        os_.append(o)
        lses.append(lse)
    return _attn_out(os_, lses, x, w_out, g_post, dils)


def _scaled_bf16(w, lo, hi, scale):
    col = jnp.arange(w.shape[-1])
    return (w * jnp.where((col >= lo) & (col < hi), scale, 1.0).astype(w.dtype)).astype(BF16)


def kernel(x, norm_w, ffn_w_in, ffn_w_out, ret_w_in, ret_log1m_decay, ret_w_out, conv_w_in, conv_b_in, conv_w_dw,
           conv_b_dw, conv_ln_g, conv_ln_b, conv_w_out, conv_b_out, attn_w_in, attn_w_out):
    depth = norm_w.shape[0]
    b, seq, d = x.shape
    n_mixers = 3
    qk_w = RET_HEADS * RET_QK_DIM
    attn_gw = 3 * ATTN_HEADS * ATTN_HEAD_DIM
    ret_w_in_b = _scaled_bf16(ret_w_in, qk_w, 2 * qk_w, RET_QK_DIM ** -0.5)
    attn_col = jnp.arange(attn_w_in.shape[-1]) % attn_gw
    attn_w_in_b = (attn_w_in * jnp.where(attn_col < ATTN_HEADS * ATTN_HEAD_DIM, ATTN_HEAD_DIM ** -0.5, 1.0)
                   .astype(F32)).astype(BF16)
    row = lambda v: v.reshape(1, -1)
    for i in range(depth):
        kind, j = i % n_mixers, i // n_mixers
        g_pre, g_post = row(norm_w[i, 0]), row(norm_w[i, 1])
        if kind == 0:
            x = _retention_layer(x, g_pre, g_post, ret_w_in_b[j], ret_log1m_decay[j], ret_w_out[j].astype(BF16))
        elif kind == 1:
            x = _conv_layer(x, g_pre, g_post, conv_w_in[j].astype(BF16), row(conv_b_in[j]), conv_w_dw[j],
                            row(conv_b_dw[j]), row(conv_ln_g[j]), row(conv_ln_b[j]), conv_w_out[j].astype(BF16),
                            row(conv_b_out[j]))
        else:
            x = _attention_layer(x, g_pre, g_post, attn_w_in_b[j], attn_w_out[j].astype(BF16))
        x = _ffn(x.reshape(b * seq, d), row(norm_w[i, 2]), ffn_w_in[i].astype(BF16), ffn_w_out[i].astype(BF16),
                 row(norm_w[i, 3])).reshape(b, seq, d)
    return x
```

```python
import functools
from typing import NamedTuple

import jax
import jax.numpy as jnp
from jax import lax
from jax.experimental import pallas as pl
from jax.experimental.pallas import tpu as pltpu

F32 = jnp.float32
BF16 = jnp.bfloat16

RMS_EPS = 1e-6
LN_EPS = 1e-5
MASK_VALUE = -1e30

RET_HEADS = 4
RET_QK_DIM = 256
RET_V_DIM = 512
RET_ROPE_BASE = 10000.0
CONV_WIDTH = 31
CONV_PAD = (CONV_WIDTH - 1) // 2
ATTN_HEADS = 16
ATTN_HEAD_DIM = 64
DILATION_GROUPS = ((128, 1), (512, 4), (2048, 16))
ROPE_THETA = 500000.0
ROT_DIM = ATTN_HEAD_DIM // 4

LANES = 128
V7X_VMEM_LIMIT_BYTES = 56 * 1024 * 1024

TOKEN_TILE = 512
RET_CHUNK = 256
ATTN_Q_BLOCK = 128
ATTN_Q_STEP = 512
CONV_HALO = 16
CONV_STRIP = 128


def _params(*semantics):
    return pltpu.CompilerParams(dimension_semantics=semantics, vmem_limit_bytes=V7X_VMEM_LIMIT_BYTES)


def _resident(shape):
    zeros = (0,) * len(shape)
    return pl.BlockSpec(shape, lambda *_: zeros, pipeline_mode=pl.Buffered(1))


def _rmsnorm(x, g):
    return x * lax.rsqrt(jnp.mean(x * x, axis=-1, keepdims=True) + RMS_EPS) * g


def _silu(x):
    return x * jax.nn.sigmoid(x)


def _dot(a, b):
    return jnp.dot(a, b, preferred_element_type=F32)


def _dot_nt(a, b):
    return lax.dot_general(a, b, (((1,), (1,)), ((), ())), preferred_element_type=F32)


def _dot_tn(a, b):
    return lax.dot_general(a, b, (((0,), (0,)), ((), ())), preferred_element_type=F32)


def _ffn_tile(x, g_pre, w_in_ref, w_out_ref, g_post, hidden_chunk):
    hidden = w_out_ref.shape[0]
    xn = _rmsnorm(x, g_pre).astype(BF16)
    acc = jnp.zeros(x.shape, F32)
    for c in range(hidden // hidden_chunk):
        lo = c * hidden_chunk
        a = _dot(xn, w_in_ref[:, lo:lo + hidden_chunk])
        b = _dot(xn, w_in_ref[:, hidden + lo:hidden + lo + hidden_chunk])
        h = (_silu(a) * b).astype(BF16)
        acc = acc + _dot(h, w_out_ref[lo:lo + hidden_chunk, :])
    return x + _rmsnorm(acc, g_post)


def _ffn_hidden_chunk(hidden):
    for cand in (512, 256, 128):
        if hidden % cand == 0:
            return cand
    raise ValueError(f"hidden width {hidden} is not a multiple of {LANES}")


class Ffn(NamedTuple):
    g_pre: jax.Array
    w_in: jax.Array
    w_out: jax.Array
    g_post: jax.Array

    def specs(self):
        return [_resident(a.shape) for a in self]


def _ffn_tail(x, ffn_refs):
    g_pre_ref, w_in_ref, w_out_ref, g_post_ref = ffn_refs
    return _ffn_tile(x, g_pre_ref[...], w_in_ref, w_out_ref, g_post_ref[...], _ffn_hidden_chunk(w_out_ref.shape[0]))


def _ret_proj_kernel(x_ref, g_ref, w_ref, cos_ref, sin_ref, o_ref, *, rot_width, head_dim, plain_chunk):
    xn = _rmsnorm(x_ref[...], g_ref[...]).astype(BF16)
    cos = cos_ref[...]
    sin = sin_ref[...]
    half = head_dim // 2
    for h in range(rot_width // head_dim):
        lo = h * head_dim
        a = _dot(xn, w_ref[:, lo:lo + head_dim])
        x1 = a[:, :half]
        x2 = a[:, half:]
        o_ref[:, lo:lo + half] = (x1 * cos - x2 * sin).astype(BF16)
        o_ref[:, lo + half:lo + head_dim] = (x2 * cos + x1 * sin).astype(BF16)
    width = w_ref.shape[1]
    for lo in range(rot_width, width, plain_chunk):
        o_ref[:, lo:lo + plain_chunk] = _dot(xn, w_ref[:, lo:lo + plain_chunk]).astype(BF16)


def _ret_proj(x2d, g, w, cos, sin, seq):
    t, d = x2d.shape
    width = w.shape[1]
    tm = TOKEN_TILE
    blocks_per_seq = seq // tm
    half = RET_QK_DIM // 2
    return pl.pallas_call(
        functools.partial(_ret_proj_kernel, rot_width=2 * RET_HEADS * RET_QK_DIM, head_dim=RET_QK_DIM,
                          plain_chunk=512),
        grid=(t // tm,),
        in_specs=[
            pl.BlockSpec((tm, d), lambda i: (i, 0)),
            _resident((1, d)),
            _resident(w.shape),
            pl.BlockSpec((tm, half), lambda i: (i % blocks_per_seq, 0)),
            pl.BlockSpec((tm, half), lambda i: (i % blocks_per_seq, 0)),
        ],
        out_specs=pl.BlockSpec((tm, width), lambda i: (i, 0)),
        out_shape=jax.ShapeDtypeStruct((t, width), BF16),
        compiler_params=_params("parallel"),
        name="ret_proj",
    )(x2d, g, w, cos, sin)


def _ret_core_kernel(decay_ref, q_ref, k_ref, v_ref, o_ref, acc_ref, stf_ref, stb_ref, *, chunk):
    head = pl.program_id(1)
    seq = q_ref.shape[1]
    n_chunks = seq // chunk
    c = chunk

    def log_gamma(direction):
        log1m = jnp.full((c, 1), decay_ref[direction, head], F32)
        return jnp.log(1.0 - jnp.exp(log1m))

    lgf = log_gamma(0)
    lgb = log_gamma(1)
    idx = lax.broadcasted_iota(jnp.int32, (c, 1), 0).astype(F32)
    xi_f = jnp.exp(lgf * (idx + 1.0))
    zeta_f = jnp.exp(lgf * (c - 1.0 - idx))
    decay_f = jnp.exp(lgf * float(c))
    xi_b = jnp.exp(lgb * (c - idx))
    zeta_b = jnp.exp(lgb * idx)
    decay_b = jnp.exp(lgb * float(c))
    diff = (lax.broadcasted_iota(jnp.int32, (c, c), 0) - lax.broadcasted_iota(jnp.int32, (c, c), 1)).astype(F32)
    dmat = jnp.exp(jnp.where(diff >= 0.0, lgf, -lgb) * diff)

    def rows(i):
        return pl.ds(pl.multiple_of(i * c, c), c)

    def state_update(st, decay, kc, zeta, vc):
        kz = (kc.astype(F32) * zeta).astype(BF16)
        return st * decay + _dot_tn(kz, vc)

    stf_ref[...] = jnp.zeros(stf_ref.shape, F32)
    stb_ref[...] = jnp.zeros(stb_ref.shape, F32)

    def parts(a, b):
        ra, rb = rows(a), rows(b)
        qa, ka, va = q_ref[0, ra, :], k_ref[0, ra, :], v_ref[0, ra, :]
        qb, kb, vb = q_ref[0, rb, :], k_ref[0, rb, :], v_ref[0, rb, :]
        stf = stf_ref[...]
        stb = stb_ref[...]
        scores = _dot_nt(qa, ka)
        cross_f = _dot(qa, stf.astype(BF16))
        cross_b = _dot(qb, stb.astype(BF16))
        stf_ref[...] = state_update(stf, decay_f, ka, zeta_f, va)
        stb_ref[...] = state_update(stb, decay_b, kb, zeta_b, vb)
        intra = _dot((scores * dmat).astype(BF16), va)
        return intra + xi_f * cross_f, xi_b * cross_b

    def finish(i, o):
        o = o * lax.rsqrt(jnp.mean(o * o, axis=-1, keepdims=True) + RMS_EPS)
        o_ref[0, rows(i), :] = o.astype(o_ref.dtype)

    half = n_chunks // 2

    def approach(j, carry):
        a, b = j, n_chunks - 1 - j
        part_a, part_b = parts(a, b)
        acc_ref[rows(a), :] = part_a
        acc_ref[rows(b), :] = part_b
        return carry

    def cross(j, carry):
        a, b = half + j, half - 1 - j
        part_a, part_b = parts(a, b)
        finish(a, acc_ref[rows(a), :] + part_a)
        finish(b, acc_ref[rows(b), :] + part_b)
        return carry

    lax.fori_loop(0, half, approach, 0, unroll=True)
    lax.fori_loop(0, half, cross, 0, unroll=True)


def _ret_core(proj3d, log1m_decay):
    b, seq, _ = proj3d.shape
    dk, dv, heads = RET_QK_DIM, RET_V_DIM, RET_HEADS
    assert seq % (2 * RET_CHUNK) == 0
    k_blk0 = heads
    v_blk0 = 2 * heads * dk // dv
    return pl.pallas_call(
        functools.partial(_ret_core_kernel, chunk=RET_CHUNK),
        grid=(b, heads),
        in_specs=[
            pl.BlockSpec(memory_space=pltpu.SMEM),
            pl.BlockSpec((1, seq, dk), lambda bi, h: (bi, 0, h)),
            pl.BlockSpec((1, seq, dk), lambda bi, h: (bi, 0, k_blk0 + h)),
            pl.BlockSpec((1, seq, dv), lambda bi, h: (bi, 0, v_blk0 + h)),
        ],
        out_specs=pl.BlockSpec((1, seq, dv), lambda bi, h: (bi, 0, h)),
        out_shape=jax.ShapeDtypeStruct((b, seq, heads * dv), BF16),
        scratch_shapes=[pltpu.VMEM((seq, dv), F32), pltpu.VMEM((dk, dv), F32), pltpu.VMEM((dk, dv), F32)],
        compiler_params=_params("parallel", "parallel"),
        name="ret_core",
    )(log1m_decay, proj3d, proj3d, proj3d)


def _ret_out_kernel(o_ref, gate_ref, x_ref, w_ref, g_post_ref, *rest):
    *ffn_refs, out_ref = rest
    y = (_silu(gate_ref[...].astype(F32)) * o_ref[...].astype(F32)).astype(BF16)
    x = x_ref[...] + _rmsnorm(_dot(y, w_ref[...]), g_post_ref[...])
    out_ref[...] = _ffn_tail(x, ffn_refs)


def _ret_out(o2d, proj2d, x2d, w_out, g_post, ffn):
    t, d = x2d.shape
    vw = o2d.shape[1]
    gate_blk = proj2d.shape[1] // vw - 1
    tm = TOKEN_TILE
    return pl.pallas_call(
        _ret_out_kernel,
        grid=(t // tm,),
        in_specs=[
            pl.BlockSpec((tm, vw), lambda i: (i, 0)),
            pl.BlockSpec((tm, vw), lambda i: (i, gate_blk)),
            pl.BlockSpec((tm, d), lambda i: (i, 0)),
            _resident(w_out.shape),
            _resident((1, d)),
            *ffn.specs(),
        ],
        out_specs=pl.BlockSpec((tm, d), lambda i: (i, 0)),
        out_shape=jax.ShapeDtypeStruct((t, d), F32),
        compiler_params=_params("parallel"),
        name="ret_out_ffn",
    )(o2d, proj2d, x2d, w_out, g_post, *ffn)


def _retention_layer(x, g_pre, g_post, w_in, log1m_decay, w_out, ffn):
    b, seq, d = x.shape
    inv = 1.0 / (RET_ROPE_BASE ** jnp.linspace(0.0, 1.0, RET_QK_DIM // 2, dtype=F32))
    ang = jnp.arange(seq, dtype=F32)[:, None] * inv[None, :]
    x2d = x.reshape(b * seq, d)
    proj = _ret_proj(x2d, g_pre, w_in, jnp.cos(ang), jnp.sin(ang), seq)
    o = _ret_core(proj.reshape(b, seq, -1), log1m_decay)
    return _ret_out(o.reshape(b * seq, -1), proj, x2d, w_out, g_post, ffn).reshape(b, seq, d)


def _conv_in_kernel(x_ref, g_ref, w_ref, b_ref, u_ref, *, chunk):
    d = u_ref.shape[1]
    xn = _rmsnorm(x_ref[...], g_ref[...]).astype(BF16)
    for lo in range(0, d, chunk):
        a = _dot(xn, w_ref[:, lo:lo + chunk]) + b_ref[:, lo:lo + chunk]
        gate = _dot(xn, w_ref[:, d + lo:d + lo + chunk]) + b_ref[:, d + lo:d + lo + chunk]
        u_ref[:, lo:lo + chunk] = (a * jax.nn.sigmoid(gate)).astype(BF16)


def _conv_in(x2d, g, w_in, b_in):
    t, d = x2d.shape
    tm = TOKEN_TILE
    return pl.pallas_call(
        functools.partial(_conv_in_kernel, chunk=512),
        grid=(t // tm,),
        in_specs=[
            pl.BlockSpec((tm, d), lambda i: (i, 0)),
            _resident((1, d)),
            _resident(w_in.shape),
            _resident(b_in.shape),
        ],
        out_specs=pl.BlockSpec((tm, d), lambda i: (i, 0)),
        out_shape=jax.ShapeDtypeStruct((t, d), BF16),
        compiler_params=_params("parallel"),
        name="conv_in",
    )(x2d, g, w_in, b_in)


def _conv_out_kernel(u_prev_ref, u_ref, u_next_ref, x_ref, w_dw_ref, b_dw_ref, ln_g_ref, ln_b_ref,
                     w_ref, b_out_ref, g_post_ref, g2_ref, f_in_ref, f_out_ref, g3_ref, out_ref, ext_ref, *, strip):
    i = pl.program_id(1)
    tm = u_ref.shape[1]
    halo = u_prev_ref.shape[1]
    taps = w_dw_ref.shape[0]
    pad = (taps - 1) // 2
    prev = u_prev_ref[0].astype(F32)
    nxt = u_next_ref[0].astype(F32)
    ext_ref[0:halo, :] = jnp.where(i > 0, prev, 0.0)
    ext_ref[halo:halo + tm, :] = u_ref[0].astype(F32)
    ext_ref[halo + tm:, :] = jnp.where(i < pl.num_programs(1) - 1, nxt, 0.0)
    sub = 8
    rows_w = strip + 2 * halo
    strips = []
    for r0 in range(0, tm, strip):
        window = ext_ref[r0:r0 + rows_w, :]
        acc = jnp.zeros((strip, u_ref.shape[2]), F32) + b_dw_ref[...]
        for res in range(sub):
            shifted = window if res == 0 else pltpu.roll(window, rows_w - res, 0)
            for k in range(taps):
                off = halo - pad + k
                if off % sub == res:
                    acc = acc + shifted[off - res:off - res + strip, :] * w_dw_ref[k:k + 1, :]
        strips.append(acc)
    acc = jnp.concatenate(strips, axis=0)
    mu = jnp.mean(acc, axis=-1, keepdims=True)
    cen = acc - mu
    var = jnp.mean(cen * cen, axis=-1, keepdims=True)
    y = cen * lax.rsqrt(var + LN_EPS) * ln_g_ref[...] + ln_b_ref[...]
    m = _dot(_silu(y).astype(BF16), w_ref[...]) + b_out_ref[...]
    x = x_ref[0] + _rmsnorm(m, g_post_ref[...])
    out_ref[0] = _ffn_tail(x, (g2_ref, f_in_ref, f_out_ref, g3_ref))


def _conv_out(u, x, w_dw, b_dw, ln_g, ln_b, w_out, b_out, g_post, ffn):
    b, seq, d = x.shape
    tm = TOKEN_TILE
    halo = CONV_HALO
    r = tm // halo
    last = seq // halo - 1
    return pl.pallas_call(
        functools.partial(_conv_out_kernel, strip=CONV_STRIP),
        grid=(b, seq // tm),
        in_specs=[
            pl.BlockSpec((1, halo, d), lambda bi, i: (bi, jnp.maximum(i * r - 1, 0), 0)),
            pl.BlockSpec((1, tm, d), lambda bi, i: (bi, i, 0)),
            pl.BlockSpec((1, halo, d), lambda bi, i: (bi, jnp.minimum((i + 1) * r, last), 0)),
            pl.BlockSpec((1, tm, d), lambda bi, i: (bi, i, 0)),
            _resident(w_dw.shape),
            _resident((1, d)),
            _resident((1, d)),
            _resident((1, d)),
            _resident(w_out.shape),
            _resident((1, d)),
            _resident((1, d)),
            *ffn.specs(),
        ],
        out_specs=pl.BlockSpec((1, tm, d), lambda bi, i: (bi, i, 0)),
        out_shape=jax.ShapeDtypeStruct((b, seq, d), F32),
        scratch_shapes=[pltpu.VMEM((tm + 2 * halo, d), F32)],
        compiler_params=_params("parallel", "parallel"),
        name="conv_out_ffn",
    )(u, u, u, x, w_dw, b_dw, ln_g, ln_b, w_out, b_out, g_post, *ffn)


def _conv_layer(x, g_pre, g_post, w_in, b_in, w_dw, b_dw, ln_g, ln_b, w_out, b_out, ffn):
    b, seq, d = x.shape
    u = _conv_in(x.reshape(b * seq, d), g_pre, w_in, b_in).reshape(b, seq, d)
    return _conv_out(u, x, w_dw, b_dw, ln_g, ln_b, w_out, b_out, g_post, ffn)


def _attn_proj_kernel(x_ref, g_ref, w_ref, c_ref, s_lo_ref, s_hi_ref, o_ref, xs_ref, xp_ref, *, dil, rot_width,
                      chunk):
    tm = x_ref.shape[0]
    n = tm // dil
    xn = _rmsnorm(x_ref[...], g_ref[...])
    if dil == 1:
        xp = xn.astype(BF16)
    else:
        for cb in range(xs_ref.shape[0]):
            xs_ref[cb] = xn[:, cb * LANES:(cb + 1) * LANES]
        for r in range(dil):
            for cb in range(xs_ref.shape[0]):
                xp_ref[r * n:(r + 1) * n, cb * LANES:(cb + 1) * LANES] = (
                    xs_ref[cb, pl.ds(r, n, stride=dil), :].astype(BF16))
        xp = xp_ref[...]
    cmul = c_ref[...].reshape(tm, LANES)
    s_lo = s_lo_ref[...].reshape(tm, LANES)
    s_hi = s_hi_ref[...].reshape(tm, LANES)
    half = ROT_DIM // 2
    width = w_ref.shape[1]
    for lo in range(0, width, chunk):
        a = _dot(xp, w_ref[:, lo:lo + chunk])
        for sub in range(0, chunk, LANES):
            piece = a[:, sub:sub + LANES]
            if lo < rot_width:
                piece = (piece * cmul + pltpu.roll(piece, LANES - half, 1) * s_lo
                         + pltpu.roll(piece, half, 1) * s_hi)
            piece = piece.astype(BF16)
            for r in range(dil):
                o_ref[0, r, :, lo + sub:lo + sub + LANES] = piece[r * n:(r + 1) * n, :]


def _attn_proj(x2d, g, w, tables, dil, seq):
    t, d = x2d.shape
    width = w.shape[1]
    tm = TOKEN_TILE
    n = tm // dil
    blocks_per_seq = seq // tm
    b = t // seq
    sub_len = seq // dil
    table_spec = pl.BlockSpec((dil, n, LANES), lambda i: (0, i % blocks_per_seq, 0))
    return pl.pallas_call(
        functools.partial(_attn_proj_kernel, dil=dil, rot_width=2 * ATTN_HEADS * ATTN_HEAD_DIM, chunk=512),
        grid=(t // tm,),
        in_specs=[
            pl.BlockSpec((tm, d), lambda i: (i, 0)),
            _resident((1, d)),
            _resident(w.shape),
            table_spec, table_spec, table_spec,
        ],
        out_specs=pl.BlockSpec((1, dil, n, width), lambda i: (i // blocks_per_seq, 0, i % blocks_per_seq, 0)),
        out_shape=jax.ShapeDtypeStruct((b, dil, sub_len, width), BF16),
        scratch_shapes=[pltpu.VMEM((d // LANES, tm, LANES), F32), pltpu.VMEM((tm, d), BF16)],
        compiler_params=_params("parallel"),
        name=f"attn_proj_d{dil}",
    )(x2d, g, w, *tables)


def _band_attn_kernel(q_ref, kp_ref, kc_ref, kn_ref, vp_ref, vc_ref, vn_ref, o_ref, stat_ref, bias_ref, *,
                      heads, head_dim, radius, sub_len, tq):
    i = pl.program_id(1)
    rows_step = q_ref.shape[1]
    n_sub = rows_step // tq
    nk = tq + 2 * radius
    row = lax.broadcasted_iota(jnp.int32, (tq, nk), 0)
    col = lax.broadcasted_iota(jnp.int32, (tq, nk), 1)
    off = col - row
    band = (off >= 0) & (off <= 2 * radius)
    lane = lax.broadcasted_iota(jnp.int32, (tq, LANES), 1)
    first = lane < head_dim
    keep_first = first.astype(F32).astype(BF16)
    keep_second = (1.0 - first.astype(F32)).astype(BF16)
    ones = jnp.ones((nk, LANES), BF16)

    def window(prev_ref, cur_ref, next_ref, r0):
        before = prev_ref[0] if r0 == 0 else cur_ref[0, r0 - radius:r0, :]
        after = next_ref[0] if r0 + tq == rows_step else cur_ref[0, r0 + tq:r0 + tq + radius, :]
        return jnp.concatenate([before, cur_ref[0, r0:r0 + tq, :], after], axis=0)

    for sub in range(n_sub):
        r0 = sub * tq
        kwin = window(kp_ref, kc_ref, kn_ref, r0)
        vwin = window(vp_ref, vc_ref, vn_ref, r0)
        kpos = i * rows_step + r0 - radius + col
        bias_ref[sub] = jnp.where(band & (kpos >= 0) & (kpos < sub_len), 0.0, MASK_VALUE)
        stat = jnp.zeros((tq, LANES), F32)
        for pair in range(heads // 2):
            cols = slice(pair * LANES, (pair + 1) * LANES)
            qp = q_ref[0, r0:r0 + tq, cols]
            s2 = _dot_nt(jnp.concatenate([qp * keep_first, qp * keep_second], axis=0), kwin[:, cols])
            outs = []
            for k in range(2):
                h = 2 * pair + k
                s = s2[k * tq:(k + 1) * tq] + bias_ref[sub]
                mx = jnp.max(s, axis=-1, keepdims=True)
                p = jnp.exp(s - mx).astype(BF16)
                pv = _dot(p, jnp.concatenate([vwin[:, cols], ones], axis=1))
                outs.append(pv[:, :LANES])
                stat = jnp.where(lane == h, mx, stat)
                stat = jnp.where(lane == heads + h, pv[:, LANES:], stat)
            o_ref[0, r0:r0 + tq, cols] = jnp.where(first, outs[0], outs[1]).astype(o_ref.dtype)
        stat_ref[0, r0:r0 + tq, :] = stat


def _band_attn(qkv, dil, radius):
    assert 2 * ATTN_HEADS <= LANES and 2 * ATTN_HEAD_DIM == LANES
    b, _, sub_len, w3 = qkv.shape
    width = w3 // 3
    tq = ATTN_Q_BLOCK
    rows_step = min(ATTN_Q_STEP, sub_len)
    nstep = sub_len // rows_step
    assert radius <= tq and rows_step % tq == 0 and sub_len % rows_step == 0 and rows_step % radius == 0
    flat = qkv.reshape(b * dil, sub_len, w3)
    per_step = rows_step // radius
    last_halo = sub_len // radius - 1

    def body(col):
        return pl.BlockSpec((1, rows_step, width), lambda s, i: (s, i, col))

    def before(col):
        return pl.BlockSpec((1, radius, width), lambda s, i: (s, jnp.maximum(i * per_step - 1, 0), col))

    def after(col):
        return pl.BlockSpec((1, radius, width), lambda s, i: (s, jnp.minimum((i + 1) * per_step, last_halo), col))

    return pl.pallas_call(
        functools.partial(_band_attn_kernel, heads=ATTN_HEADS, head_dim=ATTN_HEAD_DIM, radius=radius,
                          sub_len=sub_len, tq=tq),
        grid=(b * dil, nstep),
        in_specs=[body(0), before(1), body(1), after(1), before(2), body(2), after(2)],
        out_specs=[
            pl.BlockSpec((1, rows_step, width), lambda s, i: (s // dil, i, s % dil)),
            pl.BlockSpec((1, rows_step, LANES), lambda s, i: (s // dil, i, s % dil)),
        ],
        out_shape=[
            jax.ShapeDtypeStruct((b, sub_len, dil * width), BF16),
            jax.ShapeDtypeStruct((b, sub_len, dil * LANES), F32),
        ],
        scratch_shapes=[pltpu.VMEM((rows_step // tq, tq, tq + 2 * radius), F32)],
        compiler_params=_params("parallel", "parallel"),
        name=f"band_attn_d{dil}",
    )(flat, flat, flat, flat, flat, flat, flat)


def _attn_out_kernel(*refs, dils, head_dim):
    n_groups = len(dils)
    o_refs = refs[:n_groups]
    lse_refs = refs[n_groups:2 * n_groups]
    x_ref, w_ref, g_post_ref, *ffn_refs, out_ref = refs[2 * n_groups:2 * n_groups + 8]
    scratch = refs[2 * n_groups + 8:]
    tm, d = x_ref.shape[1], x_ref.shape[2]

    def natural(ref, dil, width, buf):
        if dil == 1:
            return ref[0].astype(F32)
        n = tm // dil
        for r in range(dil):
            for cb in range(width // LANES):
                lo = r * width + cb * LANES
                buf[cb, pl.ds(r, n, stride=dil), :] = ref[0, :, lo:lo + LANES].astype(F32)
        return jnp.concatenate([buf[cb] for cb in range(width // LANES)], axis=1)

    outs, lses = [], []
    k = 0
    for g, dil in enumerate(dils):
        if dil == 1:
            outs.append(natural(o_refs[g], 1, d, None))
            lses.append(natural(lse_refs[g], 1, LANES, None))
        else:
            outs.append(natural(o_refs[g], dil, d, scratch[k]))
            lses.append(natural(lse_refs[g], dil, LANES, scratch[k + 1]))
            k += 2
    heads = d // head_dim
    head_lane = lax.broadcasted_iota(jnp.int32, (tm, LANES), 1) < heads
    mx = functools.reduce(jnp.maximum, lses)
    es = [jnp.exp(l - mx) for l in lses]
    dens = [pltpu.roll(l, LANES - heads, 1) for l in lses]
    tot = functools.reduce(lambda a, b_: a + b_, [e * dn for e, dn in zip(es, dens)])
    tot = jnp.where(head_lane, tot, 1.0)
    expand = (lax.broadcasted_iota(jnp.int32, (2 * LANES, d), 1) // head_dim
              == lax.broadcasted_iota(jnp.int32, (2 * LANES, d), 0) % LANES).astype(BF16)
    mixed = jnp.zeros((tm, d), F32)
    for e, o in zip(es, outs):
        wgt = jnp.where(head_lane, e / tot, 0.0)
        hi = wgt.astype(BF16)
        lo = (wgt - hi.astype(F32)).astype(BF16)
        mixed = mixed + _dot(jnp.concatenate([hi, lo], axis=1), expand) * o
    m = _dot(mixed.astype(BF16), w_ref[...])
    x = x_ref[0] + _rmsnorm(m, g_post_ref[...])
    out_ref[0] = _ffn_tail(x, ffn_refs)


def _attn_out(os_, lses, x, w_out, g_post, dils, ffn):
    b, seq, d = x.shape
    tm = TOKEN_TILE
    in_specs = []
    for dil in dils:
        in_specs.append(pl.BlockSpec((1, tm // dil, dil * d), lambda bi, i: (bi, i, 0)))
    for dil in dils:
        in_specs.append(pl.BlockSpec((1, tm // dil, dil * LANES), lambda bi, i: (bi, i, 0)))
    in_specs += [pl.BlockSpec((1, tm, d), lambda bi, i: (bi, i, 0)), _resident(w_out.shape), _resident((1, d)),
                 *ffn.specs()]
    scratch = []
    for dil in dils:
        if dil > 1:
            scratch += [pltpu.VMEM((d // LANES, tm, LANES), F32), pltpu.VMEM((1, tm, LANES), F32)]
    return pl.pallas_call(
        functools.partial(_attn_out_kernel, dils=dils, head_dim=ATTN_HEAD_DIM),
        grid=(b, seq // tm),
        in_specs=in_specs,
        out_specs=pl.BlockSpec((1, tm, d), lambda bi, i: (bi, i, 0)),
        out_shape=jax.ShapeDtypeStruct((b, seq, d), F32),
        scratch_shapes=scratch,
        compiler_params=_params("parallel", "parallel"),
        name="attn_out_ffn",
    )(*os_, *lses, x, w_out, g_post, *ffn)


def _rotary_tables(seq, dil):
    inv = ROPE_THETA ** (-jnp.arange(0, ROT_DIM, 2, dtype=F32) / ROT_DIM)
    ang = jnp.arange(seq, dtype=F32)[:, None] * inv[None, :]
    cos, sin = jnp.cos(ang), jnp.sin(ang)
    half = ROT_DIM // 2
    rest = ATTN_HEAD_DIM - ROT_DIM
    ones = jnp.ones((seq, rest), F32)
    zeros_h = jnp.zeros((seq, half), F32)
    zeros_r = jnp.zeros((seq, rest), F32)
    per_head = (
        jnp.concatenate([cos, cos, ones], axis=1),
        jnp.concatenate([-sin, zeros_h, zeros_r], axis=1),
        jnp.concatenate([zeros_h, sin, zeros_r], axis=1),
    )
    reps = LANES // ATTN_HEAD_DIM
    out = []
    for tbl in per_head:
        tbl = jnp.tile(tbl, (1, reps))
        out.append(tbl.reshape(seq // dil, dil, LANES).transpose(1, 0, 2))
    return out


def _attention_layer(x, g_pre, g_post, w_in, w_out, ffn):
    b, seq, d = x.shape
    x2d = x.reshape(b * seq, d)
    group_width = w_in.shape[1] // len(DILATION_GROUPS)
    dils = tuple(dil for _, dil in DILATION_GROUPS)
    os_, lses = [], []
    for g, (window, dil) in enumerate(DILATION_GROUPS):
        w_g = w_in[:, g * group_width:(g + 1) * group_width]
        qkv = _attn_proj(x2d, g_pre, w_g, _rotary_tables(seq, dil), dil, seq)
        o, lse = _band_attn(qkv, dil, window // (2 * dil))
        os_.append(o)
        lses.append(lse)
    return _attn_out(os_, lses, x, w_out, g_post, dils, ffn)


def _scaled_bf16(w, lo, hi, scale):
    col = jnp.arange(w.shape[-1])
    return (w * jnp.where((col >= lo) & (col < hi), scale, 1.0).astype(w.dtype)).astype(BF16)


def kernel(x, norm_w, ffn_w_in, ffn_w_out, ret_w_in, ret_log1m_decay, ret_w_out, conv_w_in, conv_b_in, conv_w_dw,
           conv_b_dw, conv_ln_g, conv_ln_b, conv_w_out, conv_b_out, attn_w_in, attn_w_out):
    depth = norm_w.shape[0]
    b, seq, d = x.shape
    n_mixers = 3
    qk_w = RET_HEADS * RET_QK_DIM
    attn_gw = 3 * ATTN_HEADS * ATTN_HEAD_DIM
    ret_w_in_b = _scaled_bf16(ret_w_in, qk_w, 2 * qk_w, RET_QK_DIM ** -0.5)
    attn_col = jnp.arange(attn_w_in.shape[-1]) % attn_gw
    attn_w_in_b = (attn_w_in * jnp.where(attn_col < ATTN_HEADS * ATTN_HEAD_DIM, ATTN_HEAD_DIM ** -0.5, 1.0)
                   .astype(F32)).astype(BF16)
    row = lambda v: v.reshape(1, -1)
    for i in range(depth):
        kind, j = i % n_mixers, i // n_mixers
        g_pre, g_post = row(norm_w[i, 0]), row(norm_w[i, 1])
        ffn = Ffn(row(norm_w[i, 2]), ffn_w_in[i].astype(BF16), ffn_w_out[i].astype(BF16), row(norm_w[i, 3]))
        if kind == 0:
            x = _retention_layer(x, g_pre, g_post, ret_w_in_b[j], ret_log1m_decay[j], ret_w_out[j].astype(BF16), ffn)
        elif kind == 1:
            x = _conv_layer(x, g_pre, g_post, conv_w_in[j].astype(BF16), row(conv_b_in[j]), conv_w_dw[j],
                            row(conv_b_dw[j]), row(conv_ln_g[j]), row(conv_ln_b[j]), conv_w_out[j].astype(BF16),
                            row(conv_b_out[j]), ffn)
        else:
            x = _attention_layer(x, g_pre, g_post, attn_w_in_b[j], attn_w_out[j].astype(BF16), ffn)
    return x
```

```python
import functools
from typing import NamedTuple

import jax
import jax.numpy as jnp
from jax import lax
from jax.experimental import pallas as pl
from jax.experimental.pallas import tpu as pltpu

F32 = jnp.float32
BF16 = jnp.bfloat16

RMS_EPS = 1e-6
LN_EPS = 1e-5
MASK_VALUE = -1e30

RET_HEADS = 4
RET_QK_DIM = 256
RET_V_DIM = 512
RET_ROPE_BASE = 10000.0
CONV_WIDTH = 31
CONV_PAD = (CONV_WIDTH - 1) // 2
ATTN_HEADS = 16
ATTN_HEAD_DIM = 64
DILATION_GROUPS = ((128, 1), (512, 4), (2048, 16))
ROPE_THETA = 500000.0
ROT_DIM = ATTN_HEAD_DIM // 4

LANES = 128
SUBLANES = 8
V7X_VMEM_LIMIT_BYTES = 56 * 1024 * 1024

TOKEN_TILE = 512
RET_CHUNK = 256
ATTN_Q_BLOCK = 128
ATTN_Q_STEP = 512
CONV_HALO = 16
CONV_STRIP = 128


def _params(*semantics):
    return pltpu.CompilerParams(dimension_semantics=semantics, vmem_limit_bytes=V7X_VMEM_LIMIT_BYTES)


def _resident(shape):
    zeros = (0,) * len(shape)
    return pl.BlockSpec(shape, lambda *_: zeros, pipeline_mode=pl.Buffered(1))


def _rmsnorm(x, g):
    return x * lax.rsqrt(jnp.mean(x * x, axis=-1, keepdims=True) + RMS_EPS) * g


def _silu(x):
    return x * jax.nn.sigmoid(x)


def _dot(a, b):
    return jnp.dot(a, b, preferred_element_type=F32)


def _dot_nt(a, b):
    return lax.dot_general(a, b, (((1,), (1,)), ((), ())), preferred_element_type=F32)


def _dot_tn(a, b):
    return lax.dot_general(a, b, (((0,), (0,)), ((), ())), preferred_element_type=F32)


def _ffn_tile(x, g_pre, w_in_ref, w_out_ref, g_post, hidden_chunk):
    hidden = w_out_ref.shape[0]
    xn = _rmsnorm(x, g_pre).astype(BF16)
    acc = jnp.zeros(x.shape, F32)
    for c in range(hidden // hidden_chunk):
        lo = c * hidden_chunk
        a = _dot(xn, w_in_ref[:, lo:lo + hidden_chunk])
        b = _dot(xn, w_in_ref[:, hidden + lo:hidden + lo + hidden_chunk])
        h = (_silu(a) * b).astype(BF16)
        acc = acc + _dot(h, w_out_ref[lo:lo + hidden_chunk, :])
    return x + _rmsnorm(acc, g_post)


def _ffn_hidden_chunk(hidden):
    for cand in (512, 256, 128):
        if hidden % cand == 0:
            return cand
    raise ValueError(f"hidden width {hidden} is not a multiple of {LANES}")


class Ffn(NamedTuple):
    g_pre: jax.Array
    w_in: jax.Array
    w_out: jax.Array
    g_post: jax.Array

    def specs(self):
        return [_resident(a.shape) for a in self]


def _ffn_tail(x, ffn_refs):
    g_pre_ref, w_in_ref, w_out_ref, g_post_ref = ffn_refs
    return _ffn_tile(x, g_pre_ref[...], w_in_ref, w_out_ref, g_post_ref[...], _ffn_hidden_chunk(w_out_ref.shape[0]))


def _ret_proj_kernel(x_ref, g_ref, w_ref, cos_ref, sin_ref, o_ref, *, rot_width, head_dim, plain_chunk):
    xn = _rmsnorm(x_ref[...], g_ref[...]).astype(BF16)
    cos = cos_ref[...]
    sin = sin_ref[...]
    half = head_dim // 2
    for h in range(rot_width // head_dim):
        lo = h * head_dim
        a = _dot(xn, w_ref[:, lo:lo + head_dim])
        x1 = a[:, :half]
        x2 = a[:, half:]
        o_ref[:, lo:lo + half] = (x1 * cos - x2 * sin).astype(BF16)
        o_ref[:, lo + half:lo + head_dim] = (x2 * cos + x1 * sin).astype(BF16)
    width = w_ref.shape[1]
    for lo in range(rot_width, width, plain_chunk):
        o_ref[:, lo:lo + plain_chunk] = _dot(xn, w_ref[:, lo:lo + plain_chunk]).astype(BF16)


def _ret_proj(x2d, g, w, cos, sin, seq):
    t, d = x2d.shape
    width = w.shape[1]
    tm = TOKEN_TILE
    blocks_per_seq = seq // tm
    half = RET_QK_DIM // 2
    return pl.pallas_call(
        functools.partial(_ret_proj_kernel, rot_width=2 * RET_HEADS * RET_QK_DIM, head_dim=RET_QK_DIM,
                          plain_chunk=512),
        grid=(t // tm,),
        in_specs=[
            pl.BlockSpec((tm, d), lambda i: (i, 0)),
            _resident((1, d)),
            _resident(w.shape),
            pl.BlockSpec((tm, half), lambda i: (i % blocks_per_seq, 0)),
            pl.BlockSpec((tm, half), lambda i: (i % blocks_per_seq, 0)),
        ],
        out_specs=pl.BlockSpec((tm, width), lambda i: (i, 0)),
        out_shape=jax.ShapeDtypeStruct((t, width), BF16),
        compiler_params=_params("parallel"),
        name="ret_proj",
    )(x2d, g, w, cos, sin)


def _ret_core_kernel(decay_ref, q_ref, k_ref, v_ref, o_ref, acc_ref, stf_ref, stb_ref, *, chunk):
    head = pl.program_id(1)
    seq = q_ref.shape[1]
    n_chunks = seq // chunk
    c = chunk

    def log_gamma(direction):
        log1m = jnp.full((c, 1), decay_ref[direction, head], F32)
        return jnp.log(1.0 - jnp.exp(log1m))

    lgf = log_gamma(0)
    lgb = log_gamma(1)
    idx = lax.broadcasted_iota(jnp.int32, (c, 1), 0).astype(F32)
    xi_f = jnp.exp(lgf * (idx + 1.0))
    zeta_f = jnp.exp(lgf * (c - 1.0 - idx))
    decay_f = jnp.exp(lgf * float(c))
    xi_b = jnp.exp(lgb * (c - idx))
    zeta_b = jnp.exp(lgb * idx)
    decay_b = jnp.exp(lgb * float(c))
    diff = (lax.broadcasted_iota(jnp.int32, (c, c), 0) - lax.broadcasted_iota(jnp.int32, (c, c), 1)).astype(F32)
    dmat = jnp.exp(jnp.where(diff >= 0.0, lgf, -lgb) * diff)

    def rows(i):
        return pl.ds(pl.multiple_of(i * c, c), c)

    def state_update(st, decay, kc, zeta, vc):
        kz = (kc.astype(F32) * zeta).astype(BF16)
        return st * decay + _dot_tn(kz, vc)

    stf_ref[...] = jnp.zeros(stf_ref.shape, F32)
    stb_ref[...] = jnp.zeros(stb_ref.shape, F32)

    def parts(a, b):
        ra, rb = rows(a), rows(b)
        qa, ka, va = q_ref[0, ra, :], k_ref[0, ra, :], v_ref[0, ra, :]
        qb, kb, vb = q_ref[0, rb, :], k_ref[0, rb, :], v_ref[0, rb, :]
        stf = stf_ref[...]
        stb = stb_ref[...]
        scores = _dot_nt(qa, ka)
        cross_f = _dot(qa, stf.astype(BF16))
        cross_b = _dot(qb, stb.astype(BF16))
        stf_ref[...] = state_update(stf, decay_f, ka, zeta_f, va)
        stb_ref[...] = state_update(stb, decay_b, kb, zeta_b, vb)
        intra = _dot((scores * dmat).astype(BF16), va)
        return intra + xi_f * cross_f, xi_b * cross_b

    def finish(i, o):
        o = o * lax.rsqrt(jnp.mean(o * o, axis=-1, keepdims=True) + RMS_EPS)
        o_ref[0, rows(i), :] = o.astype(o_ref.dtype)

    half = n_chunks // 2

    def approach(j, carry):
        a, b = j, n_chunks - 1 - j
        part_a, part_b = parts(a, b)
        acc_ref[rows(a), :] = part_a
        acc_ref[rows(b), :] = part_b
        return carry

    def cross(j, carry):
        a, b = half + j, half - 1 - j
        part_a, part_b = parts(a, b)
        finish(a, acc_ref[rows(a), :] + part_a)
        finish(b, acc_ref[rows(b), :] + part_b)
        return carry

    lax.fori_loop(0, half, approach, 0, unroll=True)
    lax.fori_loop(0, half, cross, 0, unroll=True)


def _ret_core(proj3d, log1m_decay):
    b, seq, _ = proj3d.shape
    dk, dv, heads = RET_QK_DIM, RET_V_DIM, RET_HEADS
    assert seq % (2 * RET_CHUNK) == 0
    k_blk0 = heads
    v_blk0 = 2 * heads * dk // dv
    return pl.pallas_call(
        functools.partial(_ret_core_kernel, chunk=RET_CHUNK),
        grid=(b, heads),
        in_specs=[
            pl.BlockSpec(memory_space=pltpu.SMEM),
            pl.BlockSpec((1, seq, dk), lambda bi, h: (bi, 0, h)),
            pl.BlockSpec((1, seq, dk), lambda bi, h: (bi, 0, k_blk0 + h)),
            pl.BlockSpec((1, seq, dv), lambda bi, h: (bi, 0, v_blk0 + h)),
        ],
        out_specs=pl.BlockSpec((1, seq, dv), lambda bi, h: (bi, 0, h)),
        out_shape=jax.ShapeDtypeStruct((b, seq, heads * dv), BF16),
        scratch_shapes=[pltpu.VMEM((seq, dv), F32), pltpu.VMEM((dk, dv), F32), pltpu.VMEM((dk, dv), F32)],
        compiler_params=_params("parallel", "parallel"),
        name="ret_core",
    )(log1m_decay, proj3d, proj3d, proj3d)


def _ret_out_kernel(o_ref, gate_ref, x_ref, w_ref, g_post_ref, *rest):
    *ffn_refs, out_ref = rest
    y = (_silu(gate_ref[...].astype(F32)) * o_ref[...].astype(F32)).astype(BF16)
    x = x_ref[...] + _rmsnorm(_dot(y, w_ref[...]), g_post_ref[...])
    out_ref[...] = _ffn_tail(x, ffn_refs)


def _ret_out(o2d, proj2d, x2d, w_out, g_post, ffn):
    t, d = x2d.shape
    vw = o2d.shape[1]
    gate_blk = proj2d.shape[1] // vw - 1
    tm = TOKEN_TILE
    return pl.pallas_call(
        _ret_out_kernel,
        grid=(t // tm,),
        in_specs=[
            pl.BlockSpec((tm, vw), lambda i: (i, 0)),
            pl.BlockSpec((tm, vw), lambda i: (i, gate_blk)),
            pl.BlockSpec((tm, d), lambda i: (i, 0)),
            _resident(w_out.shape),
            _resident((1, d)),
            *ffn.specs(),
        ],
        out_specs=pl.BlockSpec((tm, d), lambda i: (i, 0)),
        out_shape=jax.ShapeDtypeStruct((t, d), F32),
        compiler_params=_params("parallel"),
        name="ret_out_ffn",
    )(o2d, proj2d, x2d, w_out, g_post, *ffn)


def _retention_layer(x, g_pre, g_post, w_in, log1m_decay, w_out, ffn):
    b, seq, d = x.shape
    inv = 1.0 / (RET_ROPE_BASE ** jnp.linspace(0.0, 1.0, RET_QK_DIM // 2, dtype=F32))
    ang = jnp.arange(seq, dtype=F32)[:, None] * inv[None, :]
    x2d = x.reshape(b * seq, d)
    proj = _ret_proj(x2d, g_pre, w_in, jnp.cos(ang), jnp.sin(ang), seq)
    o = _ret_core(proj.reshape(b, seq, -1), log1m_decay)
    return _ret_out(o.reshape(b * seq, -1), proj, x2d, w_out, g_post, ffn).reshape(b, seq, d)


def _conv_in_kernel(x_ref, g_ref, w_ref, b_ref, u_ref, *, chunk):
    d = u_ref.shape[1]
    xn = _rmsnorm(x_ref[...], g_ref[...]).astype(BF16)
    for lo in range(0, d, chunk):
        a = _dot(xn, w_ref[:, lo:lo + chunk]) + b_ref[:, lo:lo + chunk]
        gate = _dot(xn, w_ref[:, d + lo:d + lo + chunk]) + b_ref[:, d + lo:d + lo + chunk]
        u_ref[:, lo:lo + chunk] = (a * jax.nn.sigmoid(gate)).astype(BF16)


def _conv_in(x2d, g, w_in, b_in):
    t, d = x2d.shape
    tm = TOKEN_TILE
    return pl.pallas_call(
        functools.partial(_conv_in_kernel, chunk=512),
        grid=(t // tm,),
        in_specs=[
            pl.BlockSpec((tm, d), lambda i: (i, 0)),
            _resident((1, d)),
            _resident(w_in.shape),
            _resident(b_in.shape),
        ],
        out_specs=pl.BlockSpec((tm, d), lambda i: (i, 0)),
        out_shape=jax.ShapeDtypeStruct((t, d), BF16),
        compiler_params=_params("parallel"),
        name="conv_in",
    )(x2d, g, w_in, b_in)


def _conv_out_kernel(u_prev_ref, u_ref, u_next_ref, x_ref, w_dw_ref, b_dw_ref, ln_g_ref, ln_b_ref,
                     w_ref, b_out_ref, g_post_ref, g2_ref, f_in_ref, f_out_ref, g3_ref, shift_ref, out_ref, ext_ref, *,
                     strip):
    i = pl.program_id(1)
    tm = u_ref.shape[1]
    halo = u_prev_ref.shape[1]
    taps = w_dw_ref.shape[0]
    pad = (taps - 1) // 2
    zero = jnp.zeros((halo, u_ref.shape[2]), BF16)
    ext_ref[0:halo, :] = jnp.where(i > 0, u_prev_ref[0], zero)
    ext_ref[halo:halo + tm, :] = u_ref[0]
    ext_ref[halo + tm:, :] = jnp.where(i < pl.num_programs(1) - 1, u_next_ref[0], zero)
    rows_w = strip + 2 * halo
    strips = []
    for r0 in range(0, tm, strip):
        window = ext_ref[r0:r0 + rows_w, :]
        moved = _dot(shift_ref[...], window)
        aligned = window.astype(F32)
        acc = jnp.zeros((strip, u_ref.shape[2]), F32) + b_dw_ref[...]
        for res in range(SUBLANES):
            for k in range(taps):
                off = halo - pad + k
                if off % SUBLANES == res:
                    if res == 0:
                        src = aligned[off:off + strip, :]
                    else:
                        lo = (res - 1) * rows_w + off - res
                        src = moved[lo:lo + strip, :]
                    acc = acc + src * w_dw_ref[k:k + 1, :]
        strips.append(acc)
    acc = jnp.concatenate(strips, axis=0)
    mu = jnp.mean(acc, axis=-1, keepdims=True)
    cen = acc - mu
    var = jnp.mean(cen * cen, axis=-1, keepdims=True)
    y = cen * lax.rsqrt(var + LN_EPS) * ln_g_ref[...] + ln_b_ref[...]
    m = _dot(_silu(y).astype(BF16), w_ref[...]) + b_out_ref[...]
    x = x_ref[0] + _rmsnorm(m, g_post_ref[...])
    out_ref[0] = _ffn_tail(x, (g2_ref, f_in_ref, f_out_ref, g3_ref))


def _conv_out(u, x, w_dw, b_dw, ln_g, ln_b, w_out, b_out, g_post, ffn):
    b, seq, d = x.shape
    tm = TOKEN_TILE
    halo = CONV_HALO
    r = tm // halo
    last = seq // halo - 1
    rows_w = CONV_STRIP + 2 * halo
    out_row = lax.broadcasted_iota(jnp.int32, ((SUBLANES - 1) * rows_w, rows_w), 0)
    in_row = lax.broadcasted_iota(jnp.int32, ((SUBLANES - 1) * rows_w, rows_w), 1)
    shift = (in_row == out_row % rows_w + out_row // rows_w + 1).astype(BF16)
    return pl.pallas_call(
        functools.partial(_conv_out_kernel, strip=CONV_STRIP),
        grid=(b, seq // tm),
        in_specs=[
            pl.BlockSpec((1, halo, d), lambda bi, i: (bi, jnp.maximum(i * r - 1, 0), 0)),
            pl.BlockSpec((1, tm, d), lambda bi, i: (bi, i, 0)),
            pl.BlockSpec((1, halo, d), lambda bi, i: (bi, jnp.minimum((i + 1) * r, last), 0)),
            pl.BlockSpec((1, tm, d), lambda bi, i: (bi, i, 0)),
            _resident(w_dw.shape),
            _resident((1, d)),
            _resident((1, d)),
            _resident((1, d)),
            _resident(w_out.shape),
            _resident((1, d)),
            _resident((1, d)),
            *ffn.specs(),
            _resident(shift.shape),
        ],
        out_specs=pl.BlockSpec((1, tm, d), lambda bi, i: (bi, i, 0)),
        out_shape=jax.ShapeDtypeStruct((b, seq, d), F32),
        scratch_shapes=[pltpu.VMEM((tm + 2 * halo, d), BF16)],
        compiler_params=_params("parallel", "parallel"),
        name="conv_out_ffn",
    )(u, u, u, x, w_dw, b_dw, ln_g, ln_b, w_out, b_out, g_post, *ffn, shift)


def _conv_layer(x, g_pre, g_post, w_in, b_in, w_dw, b_dw, ln_g, ln_b, w_out, b_out, ffn):
    b, seq, d = x.shape
    u = _conv_in(x.reshape(b * seq, d), g_pre, w_in, b_in).reshape(b, seq, d)
    return _conv_out(u, x, w_dw, b_dw, ln_g, ln_b, w_out, b_out, g_post, ffn)


def _attn_proj_kernel(x_ref, g_ref, w_ref, c_ref, s_lo_ref, s_hi_ref, o_ref, xs_ref, xp_ref, *, dil, rot_width,
                      chunk):
    tm = x_ref.shape[0]
    n = tm // dil
    xn = _rmsnorm(x_ref[...], g_ref[...])
    if dil == 1:
        xp = xn.astype(BF16)
    else:
        for cb in range(xs_ref.shape[0]):
            xs_ref[cb] = xn[:, cb * LANES:(cb + 1) * LANES]
        for r in range(dil):
            for cb in range(xs_ref.shape[0]):
                xp_ref[r * n:(r + 1) * n, cb * LANES:(cb + 1) * LANES] = (
                    xs_ref[cb, pl.ds(r, n, stride=dil), :].astype(BF16))
        xp = xp_ref[...]
    def permuted(tbl_ref):
        if dil == 1:
            return tbl_ref[...]
        return jnp.concatenate([tbl_ref[pl.ds(r, n, stride=dil), :] for r in range(dil)], axis=0)

    cmul = permuted(c_ref)
    s_lo = permuted(s_lo_ref)
    s_hi = permuted(s_hi_ref)
    half = ROT_DIM // 2
    width = w_ref.shape[1]
    for lo in range(0, width, chunk):
        a = _dot(xp, w_ref[:, lo:lo + chunk])
        for sub in range(0, chunk, LANES):
            piece = a[:, sub:sub + LANES]
            if lo < rot_width:
                piece = (piece * cmul + pltpu.roll(piece, LANES - half, 1) * s_lo
                         + pltpu.roll(piece, half, 1) * s_hi)
            piece = piece.astype(BF16)
            for r in range(dil):
                o_ref[0, r, :, lo + sub:lo + sub + LANES] = piece[r * n:(r + 1) * n, :]


def _attn_proj(x2d, g, w, tables, dil, seq):
    t, d = x2d.shape
    width = w.shape[1]
    tm = TOKEN_TILE
    n = tm // dil
    blocks_per_seq = seq // tm
    b = t // seq
    sub_len = seq // dil
    table_spec = pl.BlockSpec((tm, LANES), lambda i: (i % blocks_per_seq, 0))
    return pl.pallas_call(
        functools.partial(_attn_proj_kernel, dil=dil, rot_width=2 * ATTN_HEADS * ATTN_HEAD_DIM, chunk=512),
        grid=(t // tm,),
        in_specs=[
            pl.BlockSpec((tm, d), lambda i: (i, 0)),
            _resident((1, d)),
            _resident(w.shape),
            table_spec, table_spec, table_spec,
        ],
        out_specs=pl.BlockSpec((1, dil, n, width), lambda i: (i // blocks_per_seq, 0, i % blocks_per_seq, 0)),
        out_shape=jax.ShapeDtypeStruct((b, dil, sub_len, width), BF16),
        scratch_shapes=[pltpu.VMEM((d // LANES, tm, LANES), F32), pltpu.VMEM((tm, d), BF16)],
        compiler_params=_params("parallel"),
        name=f"attn_proj_d{dil}",
    )(x2d, g, w, *tables)


def _band_attn_kernel(q_ref, kp_ref, kc_ref, kn_ref, vp_ref, vc_ref, vn_ref, o_ref, stat_ref, bias_ref, *,
                      heads, head_dim, radius, sub_len, tq):
    i = pl.program_id(1)
    rows_step = q_ref.shape[1]
    n_sub = rows_step // tq
    nk = tq + 2 * radius
    row = lax.broadcasted_iota(jnp.int32, (tq, nk), 0)
    col = lax.broadcasted_iota(jnp.int32, (tq, nk), 1)
    off = col - row
    band = (off >= 0) & (off <= 2 * radius)
    lane = lax.broadcasted_iota(jnp.int32, (tq, LANES), 1)
    first = lane < head_dim
    keep_first = first.astype(F32).astype(BF16)
    keep_second = (1.0 - first.astype(F32)).astype(BF16)
    ones = jnp.ones((nk, LANES), BF16)

    width = q_ref.shape[2]

    def window(prev_ref, cur_ref, next_ref, sq, r0):
        before = prev_ref[sq] if r0 == 0 else cur_ref[sq, r0 - radius:r0, :]
        after = next_ref[sq] if r0 + tq == rows_step else cur_ref[sq, r0 + tq:r0 + tq + radius, :]
        return jnp.concatenate([before, cur_ref[sq, r0:r0 + tq, :], after], axis=0)

    for sub in range(n_sub):
        r0 = sub * tq
        kpos = i * rows_step + r0 - radius + col
        bias_ref[sub] = jnp.where(band & (kpos >= 0) & (kpos < sub_len), 0.0, MASK_VALUE)
        for sq in range(q_ref.shape[0]):
            kwin = window(kp_ref, kc_ref, kn_ref, sq, r0)
            vwin = window(vp_ref, vc_ref, vn_ref, sq, r0)
            stat = jnp.zeros((tq, LANES), F32)
            for pair in range(heads // 2):
                cols = slice(pair * LANES, (pair + 1) * LANES)
                qp = q_ref[sq, r0:r0 + tq, cols]
                s2 = _dot_nt(jnp.concatenate([qp * keep_first, qp * keep_second], axis=0), kwin[:, cols])
                outs = []
                for k in range(2):
                    h = 2 * pair + k
                    s = s2[k * tq:(k + 1) * tq] + bias_ref[sub]
                    mx = jnp.max(s, axis=-1, keepdims=True)
                    p = jnp.exp(s - mx).astype(BF16)
                    pv = _dot(p, jnp.concatenate([vwin[:, cols], ones], axis=1))
                    outs.append(pv[:, :LANES])
                    stat = jnp.where(lane == h, mx, stat)
                    stat = jnp.where(lane == heads + h, pv[:, LANES:], stat)
                o_lo = sq * width + pair * LANES
                o_ref[0, r0:r0 + tq, o_lo:o_lo + LANES] = jnp.where(first, outs[0], outs[1]).astype(o_ref.dtype)
            stat_ref[0, r0:r0 + tq, sq * LANES:(sq + 1) * LANES] = stat


def _band_attn(qkv, dil, radius):
    assert 2 * ATTN_HEADS <= LANES and 2 * ATTN_HEAD_DIM == LANES
    b, _, sub_len, w3 = qkv.shape
    width = w3 // 3
    tq = ATTN_Q_BLOCK
    rows_step = min(ATTN_Q_STEP, sub_len)
    nstep = sub_len // rows_step
    assert radius <= tq and rows_step % tq == 0 and sub_len % rows_step == 0 and rows_step % radius == 0
    flat = qkv.reshape(b * dil, sub_len, w3)
    per_step = rows_step // radius
    last_halo = sub_len // radius - 1
    n_seq = max(1, min(dil, ATTN_Q_STEP // rows_step))
    assert dil % n_seq == 0
    per_row = dil // n_seq

    def body(col):
        return pl.BlockSpec((n_seq, rows_step, width), lambda s, i: (s, i, col))

    def before(col):
        return pl.BlockSpec((n_seq, radius, width), lambda s, i: (s, jnp.maximum(i * per_step - 1, 0), col))

    def after(col):
        return pl.BlockSpec((n_seq, radius, width), lambda s, i: (s, jnp.minimum((i + 1) * per_step, last_halo), col))

    return pl.pallas_call(
        functools.partial(_band_attn_kernel, heads=ATTN_HEADS, head_dim=ATTN_HEAD_DIM, radius=radius,
                          sub_len=sub_len, tq=tq),
        grid=(b * per_row, nstep),
        in_specs=[body(0), before(1), body(1), after(1), before(2), body(2), after(2)],
        out_specs=[
            pl.BlockSpec((1, rows_step, n_seq * width), lambda s, i: (s // per_row, i, s % per_row)),
            pl.BlockSpec((1, rows_step, n_seq * LANES), lambda s, i: (s // per_row, i, s % per_row)),
        ],
        out_shape=[
            jax.ShapeDtypeStruct((b, sub_len, dil * width), BF16),
            jax.ShapeDtypeStruct((b, sub_len, dil * LANES), F32),
        ],
        scratch_shapes=[pltpu.VMEM((rows_step // tq, tq, tq + 2 * radius), F32)],
        compiler_params=_params("parallel", "parallel"),
        name=f"band_attn_d{dil}",
    )(flat, flat, flat, flat, flat, flat, flat)


def _attn_out_kernel(*refs, dils, head_dim):
    n_groups = len(dils)
    o_refs = refs[:n_groups]
    lse_refs = refs[n_groups:2 * n_groups]
    x_ref, w_ref, g_post_ref, *ffn_refs, out_ref = refs[2 * n_groups:2 * n_groups + 8]
    scratch = refs[2 * n_groups + 8:]
    tm, d = x_ref.shape[1], x_ref.shape[2]

    def natural(ref, dil, width, buf):
        if dil == 1:
            return ref[0].astype(F32)
        n = tm // dil
        for r in range(dil):
            for cb in range(width // LANES):
                lo = r * width + cb * LANES
                buf[cb, pl.ds(r, n, stride=dil), :] = ref[0, :, lo:lo + LANES].astype(F32)
        return jnp.concatenate([buf[cb] for cb in range(width // LANES)], axis=1)

    outs, lses = [], []
    k = 0
    for g, dil in enumerate(dils):
        if dil == 1:
            outs.append(natural(o_refs[g], 1, d, None))
            lses.append(natural(lse_refs[g], 1, LANES, None))
        else:
            outs.append(natural(o_refs[g], dil, d, scratch[k]))
            lses.append(natural(lse_refs[g], dil, LANES, scratch[k + 1]))
            k += 2
    heads = d // head_dim
    head_lane = lax.broadcasted_iota(jnp.int32, (tm, LANES), 1) < heads
    mx = functools.reduce(jnp.maximum, lses)
    es = [jnp.exp(l - mx) for l in lses]
    dens = [pltpu.roll(l, LANES - heads, 1) for l in lses]
    tot = functools.reduce(lambda a, b_: a + b_, [e * dn for e, dn in zip(es, dens)])
    tot = jnp.where(head_lane, tot, 1.0)
    expand = (lax.broadcasted_iota(jnp.int32, (2 * LANES, d), 1) // head_dim
              == lax.broadcasted_iota(jnp.int32, (2 * LANES, d), 0) % LANES).astype(BF16)
    mixed = jnp.zeros((tm, d), F32)
    for e, o in zip(es, outs):
        wgt = jnp.where(head_lane, e / tot, 0.0)
        hi = wgt.astype(BF16)
        lo = (wgt - hi.astype(F32)).astype(BF16)
        mixed = mixed + _dot(jnp.concatenate([hi, lo], axis=1), expand) * o
    m = _dot(mixed.astype(BF16), w_ref[...])
    x = x_ref[0] + _rmsnorm(m, g_post_ref[...])
    out_ref[0] = _ffn_tail(x, ffn_refs)


def _attn_out(os_, lses, x, w_out, g_post, dils, ffn):
    b, seq, d = x.shape
    tm = TOKEN_TILE
    in_specs = []
    for dil in dils:
        in_specs.append(pl.BlockSpec((1, tm // dil, dil * d), lambda bi, i: (bi, i, 0)))
    for dil in dils:
        in_specs.append(pl.BlockSpec((1, tm // dil, dil * LANES), lambda bi, i: (bi, i, 0)))
    in_specs += [pl.BlockSpec((1, tm, d), lambda bi, i: (bi, i, 0)), _resident(w_out.shape), _resident((1, d)),
                 *ffn.specs()]
    scratch = []
    for dil in dils:
        if dil > 1:
            scratch += [pltpu.VMEM((d // LANES, tm, LANES), F32), pltpu.VMEM((1, tm, LANES), F32)]
    return pl.pallas_call(
        functools.partial(_attn_out_kernel, dils=dils, head_dim=ATTN_HEAD_DIM),
        grid=(b, seq // tm),
        in_specs=in_specs,
        out_specs=pl.BlockSpec((1, tm, d), lambda bi, i: (bi, i, 0)),
        out_shape=jax.ShapeDtypeStruct((b, seq, d), F32),
        scratch_shapes=scratch,
        compiler_params=_params("parallel", "parallel"),
        name="attn_out_ffn",
    )(*os_, *lses, x, w_out, g_post, *ffn)


def _rotary_tables(seq):
    inv = ROPE_THETA ** (-jnp.arange(0, ROT_DIM, 2, dtype=F32) / ROT_DIM)
    ang = jnp.arange(seq, dtype=F32)[:, None] * inv[None, :]
    cos, sin = jnp.cos(ang), jnp.sin(ang)
    half = ROT_DIM // 2
    rest = ATTN_HEAD_DIM - ROT_DIM
    ones = jnp.ones((seq, rest), F32)
    zeros_h = jnp.zeros((seq, half), F32)
    zeros_r = jnp.zeros((seq, rest), F32)
    per_head = (
        jnp.concatenate([cos, cos, ones], axis=1),
        jnp.concatenate([-sin, zeros_h, zeros_r], axis=1),
        jnp.concatenate([zeros_h, sin, zeros_r], axis=1),
    )
    return [jnp.tile(tbl, (1, LANES // ATTN_HEAD_DIM)) for tbl in per_head]


def _attention_layer(x, g_pre, g_post, w_in, w_out, ffn):
    b, seq, d = x.shape
    x2d = x.reshape(b * seq, d)
    group_width = w_in.shape[1] // len(DILATION_GROUPS)
    dils = tuple(dil for _, dil in DILATION_GROUPS)
    os_, lses = [], []
    tables = _rotary_tables(seq)
    for g, (window, dil) in enumerate(DILATION_GROUPS):
        w_g = w_in[:, g * group_width:(g + 1) * group_width]
        qkv = _attn_proj(x2d, g_pre, w_g, tables, dil, seq)
        o, lse = _band_attn(qkv, dil, window // (2 * dil))
        os_.append(o)
        lses.append(lse)
    return _attn_out(os_, lses, x, w_out, g_post, dils, ffn)


def _scaled_bf16(w, lo, hi, scale):
    col = jnp.arange(w.shape[-1])
    return (w * jnp.where((col >= lo) & (col < hi), scale, 1.0).astype(w.dtype)).astype(BF16)


def kernel(x, norm_w, ffn_w_in, ffn_w_out, ret_w_in, ret_log1m_decay, ret_w_out, conv_w_in, conv_b_in, conv_w_dw,
           conv_b_dw, conv_ln_g, conv_ln_b, conv_w_out, conv_b_out, attn_w_in, attn_w_out):
    depth = norm_w.shape[0]
    b, seq, d = x.shape
    n_mixers = 3
    qk_w = RET_HEADS * RET_QK_DIM
    attn_gw = 3 * ATTN_HEADS * ATTN_HEAD_DIM
    ret_w_in_b = _scaled_bf16(ret_w_in, qk_w, 2 * qk_w, RET_QK_DIM ** -0.5)
    attn_col = jnp.arange(attn_w_in.shape[-1]) % attn_gw
    attn_w_in_b = (attn_w_in * jnp.where(attn_col < ATTN_HEADS * ATTN_HEAD_DIM, ATTN_HEAD_DIM ** -0.5, 1.0)
                   .astype(F32)).astype(BF16)
    row = lambda v: v.reshape(1, -1)
    for i in range(depth):
        kind, j = i % n_mixers, i // n_mixers
        g_pre, g_post = row(norm_w[i, 0]), row(norm_w[i, 1])
        ffn = Ffn(row(norm_w[i, 2]), ffn_w_in[i].astype(BF16), ffn_w_out[i].astype(BF16), row(norm_w[i, 3]))
        if kind == 0:
            x = _retention_layer(x, g_pre, g_post, ret_w_in_b[j], ret_log1m_decay[j], ret_w_out[j].astype(BF16), ffn)
        elif kind == 1:
            x = _conv_layer(x, g_pre, g_post, conv_w_in[j].astype(BF16), row(conv_b_in[j]), conv_w_dw[j],
                            row(conv_b_dw[j]), row(conv_ln_g[j]), row(conv_ln_b[j]), conv_w_out[j].astype(BF16),
                            row(conv_b_out[j]), ffn)
        else:
            x = _attention_layer(x, g_pre, g_post, attn_w_in_b[j], attn_w_out[j].astype(BF16), ffn)
    return x
```

```python
import functools
from typing import NamedTuple

import jax
import jax.numpy as jnp
from jax import lax
from jax.experimental import pallas as pl
from jax.experimental.pallas import tpu as pltpu

F32 = jnp.float32
BF16 = jnp.bfloat16

RMS_EPS = 1e-6
LN_EPS = 1e-5
MASK_VALUE = -1e30

RET_HEADS = 4
RET_QK_DIM = 256
RET_V_DIM = 512
RET_ROPE_BASE = 10000.0
CONV_WIDTH = 31
CONV_PAD = (CONV_WIDTH - 1) // 2
ATTN_HEADS = 16
ATTN_HEAD_DIM = 64
DILATION_GROUPS = ((128, 1), (512, 4), (2048, 16))
ROPE_THETA = 500000.0
ROT_DIM = ATTN_HEAD_DIM // 4

LANES = 128
SUBLANES = 8
V7X_VMEM_LIMIT_BYTES = 56 * 1024 * 1024

TOKEN_TILE = 512
RET_CHUNK = 256
ATTN_Q_BLOCK = 128
ATTN_Q_STEP = 512
CONV_HALO = 16
CONV_STRIP = 128


def _params(*semantics):
    return pltpu.CompilerParams(dimension_semantics=semantics, vmem_limit_bytes=V7X_VMEM_LIMIT_BYTES)


def _resident(shape):
    zeros = (0,) * len(shape)
    return pl.BlockSpec(shape, lambda *_: zeros, pipeline_mode=pl.Buffered(1))


def _rmsnorm(x, g):
    return x * lax.rsqrt(jnp.mean(x * x, axis=-1, keepdims=True) + RMS_EPS) * g


def _silu(x):
    return x * jax.nn.sigmoid(x)


def _dot(a, b):
    return jnp.dot(a, b, preferred_element_type=F32)


def _dot_nt(a, b):
    return lax.dot_general(a, b, (((1,), (1,)), ((), ())), preferred_element_type=F32)


def _dot_tn(a, b):
    return lax.dot_general(a, b, (((0,), (0,)), ((), ())), preferred_element_type=F32)


def _ffn_tile(x, g_pre, w_in_ref, w_out_ref, g_post, hidden_chunk):
    hidden = w_out_ref.shape[0]
    xn = _rmsnorm(x, g_pre).astype(BF16)
    acc = jnp.zeros(x.shape, F32)
    for c in range(hidden // hidden_chunk):
        lo = c * hidden_chunk
        a = _dot(xn, w_in_ref[:, lo:lo + hidden_chunk])
        b = _dot(xn, w_in_ref[:, hidden + lo:hidden + lo + hidden_chunk])
        h = (_silu(a) * b).astype(BF16)
        acc = acc + _dot(h, w_out_ref[lo:lo + hidden_chunk, :])
    return x + _rmsnorm(acc, g_post)


def _ffn_hidden_chunk(hidden):
    for cand in (512, 256, 128):
        if hidden % cand == 0:
            return cand
    raise ValueError(f"hidden width {hidden} is not a multiple of {LANES}")


class Ffn(NamedTuple):
    g_pre: jax.Array
    w_in: jax.Array
    w_out: jax.Array
    g_post: jax.Array

    def specs(self):
        return [_resident(a.shape) for a in self]


def _ffn_tail(x, ffn_refs):
    g_pre_ref, w_in_ref, w_out_ref, g_post_ref = ffn_refs
    return _ffn_tile(x, g_pre_ref[...], w_in_ref, w_out_ref, g_post_ref[...], _ffn_hidden_chunk(w_out_ref.shape[0]))


def _ret_proj_kernel(x_ref, g_ref, w_ref, cos_ref, sin_ref, o_ref, *, rot_width, head_dim, plain_chunk):
    xn = _rmsnorm(x_ref[...], g_ref[...]).astype(BF16)
    cos = cos_ref[...]
    sin = sin_ref[...]
    half = head_dim // 2
    for h in range(rot_width // head_dim):
        lo = h * head_dim
        a = _dot(xn, w_ref[:, lo:lo + head_dim])
        x1 = a[:, :half]
        x2 = a[:, half:]
        o_ref[:, lo:lo + half] = (x1 * cos - x2 * sin).astype(BF16)
        o_ref[:, lo + half:lo + head_dim] = (x2 * cos + x1 * sin).astype(BF16)
    width = w_ref.shape[1]
    for lo in range(rot_width, width, plain_chunk):
        o_ref[:, lo:lo + plain_chunk] = _dot(xn, w_ref[:, lo:lo + plain_chunk]).astype(BF16)


def _ret_proj(x2d, g, w, cos, sin, seq):
    t, d = x2d.shape
    width = w.shape[1]
    tm = TOKEN_TILE
    blocks_per_seq = seq // tm
    half = RET_QK_DIM // 2
    return pl.pallas_call(
        functools.partial(_ret_proj_kernel, rot_width=2 * RET_HEADS * RET_QK_DIM, head_dim=RET_QK_DIM,
                          plain_chunk=512),
        grid=(t // tm,),
        in_specs=[
            pl.BlockSpec((tm, d), lambda i: (i, 0)),
            _resident((1, d)),
            _resident(w.shape),
            pl.BlockSpec((tm, half), lambda i: (i % blocks_per_seq, 0)),
            pl.BlockSpec((tm, half), lambda i: (i % blocks_per_seq, 0)),
        ],
        out_specs=pl.BlockSpec((tm, width), lambda i: (i, 0)),
        out_shape=jax.ShapeDtypeStruct((t, width), BF16),
        compiler_params=_params("parallel"),
        name="ret_proj",
    )(x2d, g, w, cos, sin)


def _ret_core_kernel(decay_ref, q_ref, k_ref, v_ref, o_ref, acc_ref, stf_ref, stb_ref, *, chunk):
    head = pl.program_id(1)
    seq = q_ref.shape[1]
    n_chunks = seq // chunk
    c = chunk

    def log_gamma(direction):
        log1m = jnp.full((c, 1), decay_ref[direction, head], F32)
        return jnp.log(1.0 - jnp.exp(log1m))

    lgf = log_gamma(0)
    lgb = log_gamma(1)
    idx = lax.broadcasted_iota(jnp.int32, (c, 1), 0).astype(F32)
    xi_f = jnp.exp(lgf * (idx + 1.0))
    zeta_f = jnp.exp(lgf * (c - 1.0 - idx))
    decay_f = jnp.exp(lgf * float(c))
    xi_b = jnp.exp(lgb * (c - idx))
    zeta_b = jnp.exp(lgb * idx)
    decay_b = jnp.exp(lgb * float(c))
    diff = (lax.broadcasted_iota(jnp.int32, (c, c), 0) - lax.broadcasted_iota(jnp.int32, (c, c), 1)).astype(F32)
    dmat = jnp.exp(jnp.where(diff >= 0.0, lgf, -lgb) * diff)

    def rows(i):
        return pl.ds(pl.multiple_of(i * c, c), c)

    def state_update(st, decay, kc, zeta, vc):
        kz = (kc.astype(F32) * zeta).astype(BF16)
        return st * decay + _dot_tn(kz, vc)

    stf_ref[...] = jnp.zeros(stf_ref.shape, F32)
    stb_ref[...] = jnp.zeros(stb_ref.shape, F32)

    def parts(a, b):
        ra, rb = rows(a), rows(b)
        qa, ka, va = q_ref[0, ra, :], k_ref[0, ra, :], v_ref[0, ra, :]
        qb, kb, vb = q_ref[0, rb, :], k_ref[0, rb, :], v_ref[0, rb, :]
        stf = stf_ref[...]
        stb = stb_ref[...]
        scores = _dot_nt(qa, ka)
        cross_f = _dot(qa, stf.astype(BF16))
        cross_b = _dot(qb, stb.astype(BF16))
        stf_ref[...] = state_update(stf, decay_f, ka, zeta_f, va)
        stb_ref[...] = state_update(stb, decay_b, kb, zeta_b, vb)
        intra = _dot((scores * dmat).astype(BF16), va)
        return intra + xi_f * cross_f, xi_b * cross_b

    def finish(i, o):
        o = o * lax.rsqrt(jnp.mean(o * o, axis=-1, keepdims=True) + RMS_EPS)
        o_ref[0, rows(i), :] = o.astype(o_ref.dtype)

    half = n_chunks // 2

    def approach(j, carry):
        a, b = j, n_chunks - 1 - j
        part_a, part_b = parts(a, b)
        acc_ref[rows(a), :] = part_a
        acc_ref[rows(b), :] = part_b
        return carry

    def cross(j, carry):
        a, b = half + j, half - 1 - j
        part_a, part_b = parts(a, b)
        finish(a, acc_ref[rows(a), :] + part_a)
        finish(b, acc_ref[rows(b), :] + part_b)
        return carry

    lax.fori_loop(0, half, approach, 0, unroll=True)
    lax.fori_loop(0, half, cross, 0, unroll=True)


def _ret_core(proj3d, log1m_decay):
    b, seq, _ = proj3d.shape
    dk, dv, heads = RET_QK_DIM, RET_V_DIM, RET_HEADS
    assert seq % (2 * RET_CHUNK) == 0
    k_blk0 = heads
    v_blk0 = 2 * heads * dk // dv
    return pl.pallas_call(
        functools.partial(_ret_core_kernel, chunk=RET_CHUNK),
        grid=(b, heads),
        in_specs=[
            pl.BlockSpec(memory_space=pltpu.SMEM),
            pl.BlockSpec((1, seq, dk), lambda bi, h: (bi, 0, h)),
            pl.BlockSpec((1, seq, dk), lambda bi, h: (bi, 0, k_blk0 + h)),
            pl.BlockSpec((1, seq, dv), lambda bi, h: (bi, 0, v_blk0 + h)),
        ],
        out_specs=pl.BlockSpec((1, seq, dv), lambda bi, h: (bi, 0, h)),
        out_shape=jax.ShapeDtypeStruct((b, seq, heads * dv), BF16),
        scratch_shapes=[pltpu.VMEM((seq, dv), F32), pltpu.VMEM((dk, dv), F32), pltpu.VMEM((dk, dv), F32)],
        compiler_params=_params("parallel", "parallel"),
        name="ret_core",
    )(log1m_decay, proj3d, proj3d, proj3d)


def _ret_out_kernel(o_ref, gate_ref, x_ref, w_ref, g_post_ref, *rest):
    *ffn_refs, out_ref = rest
    y = (_silu(gate_ref[...].astype(F32)) * o_ref[...].astype(F32)).astype(BF16)
    x = x_ref[...] + _rmsnorm(_dot(y, w_ref[...]), g_post_ref[...])
    out_ref[...] = _ffn_tail(x, ffn_refs)


def _ret_out(o2d, proj2d, x2d, w_out, g_post, ffn):
    t, d = x2d.shape
    vw = o2d.shape[1]
    gate_blk = proj2d.shape[1] // vw - 1
    tm = TOKEN_TILE
    return pl.pallas_call(
        _ret_out_kernel,
        grid=(t // tm,),
        in_specs=[
            pl.BlockSpec((tm, vw), lambda i: (i, 0)),
            pl.BlockSpec((tm, vw), lambda i: (i, gate_blk)),
            pl.BlockSpec((tm, d), lambda i: (i, 0)),
            _resident(w_out.shape),
            _resident((1, d)),
            *ffn.specs(),
        ],
        out_specs=pl.BlockSpec((tm, d), lambda i: (i, 0)),
        out_shape=jax.ShapeDtypeStruct((t, d), F32),
        compiler_params=_params("parallel"),
        name="ret_out_ffn",
    )(o2d, proj2d, x2d, w_out, g_post, *ffn)


def _retention_layer(x, g_pre, g_post, w_in, log1m_decay, w_out, ffn):
    b, seq, d = x.shape
    inv = 1.0 / (RET_ROPE_BASE ** jnp.linspace(0.0, 1.0, RET_QK_DIM // 2, dtype=F32))
    ang = jnp.arange(seq, dtype=F32)[:, None] * inv[None, :]
    x2d = x.reshape(b * seq, d)
    proj = _ret_proj(x2d, g_pre, w_in, jnp.cos(ang), jnp.sin(ang), seq)
    o = _ret_core(proj.reshape(b, seq, -1), log1m_decay)
    return _ret_out(o.reshape(b * seq, -1), proj, x2d, w_out, g_post, ffn).reshape(b, seq, d)


def _conv_in_kernel(x_ref, g_ref, w_ref, b_ref, u_ref, *, chunk):
    d = u_ref.shape[1]
    xn = _rmsnorm(x_ref[...], g_ref[...]).astype(BF16)
    for lo in range(0, d, chunk):
        a = _dot(xn, w_ref[:, lo:lo + chunk]) + b_ref[:, lo:lo + chunk]
        gate = _dot(xn, w_ref[:, d + lo:d + lo + chunk]) + b_ref[:, d + lo:d + lo + chunk]
        u_ref[:, lo:lo + chunk] = (a * jax.nn.sigmoid(gate)).astype(BF16)


def _conv_in(x2d, g, w_in, b_in):
    t, d = x2d.shape
    tm = TOKEN_TILE
    return pl.pallas_call(
        functools.partial(_conv_in_kernel, chunk=512),
        grid=(t // tm,),
        in_specs=[
            pl.BlockSpec((tm, d), lambda i: (i, 0)),
            _resident((1, d)),
            _resident(w_in.shape),
            _resident(b_in.shape),
        ],
        out_specs=pl.BlockSpec((tm, d), lambda i: (i, 0)),
        out_shape=jax.ShapeDtypeStruct((t, d), BF16),
        compiler_params=_params("parallel"),
        name="conv_in",
    )(x2d, g, w_in, b_in)


def _conv_out_kernel(u_prev_ref, u_ref, u_next_ref, x_ref, w_dw_ref, b_dw_ref, ln_g_ref, ln_b_ref,
                     w_ref, b_out_ref, g_post_ref, g2_ref, f_in_ref, f_out_ref, g3_ref, shift_ref, out_ref, ext_ref, *,
                     strip):
    i = pl.program_id(1)
    tm = u_ref.shape[1]
    halo = u_prev_ref.shape[1]
    taps = w_dw_ref.shape[0]
    pad = (taps - 1) // 2
    zero = jnp.zeros((halo, u_ref.shape[2]), BF16)
    ext_ref[0:halo, :] = jnp.where(i > 0, u_prev_ref[0], zero)
    ext_ref[halo:halo + tm, :] = u_ref[0]
    ext_ref[halo + tm:, :] = jnp.where(i < pl.num_programs(1) - 1, u_next_ref[0], zero)
    rows_w = strip + 2 * halo
    strips = []
    for r0 in range(0, tm, strip):
        window = ext_ref[r0:r0 + rows_w, :]
        moved = _dot(shift_ref[...], window)
        aligned = window.astype(F32)
        acc = jnp.zeros((strip, u_ref.shape[2]), F32) + b_dw_ref[...]
        for res in range(SUBLANES):
            for k in range(taps):
                off = halo - pad + k
                if off % SUBLANES == res:
                    if res == 0:
                        src = aligned[off:off + strip, :]
                    else:
                        lo = (res - 1) * rows_w + off - res
                        src = moved[lo:lo + strip, :]
                    acc = acc + src * w_dw_ref[k:k + 1, :]
        strips.append(acc)
    acc = jnp.concatenate(strips, axis=0)
    mu = jnp.mean(acc, axis=-1, keepdims=True)
    cen = acc - mu
    var = jnp.mean(cen * cen, axis=-1, keepdims=True)
    y = cen * lax.rsqrt(var + LN_EPS) * ln_g_ref[...] + ln_b_ref[...]
    m = _dot(_silu(y).astype(BF16), w_ref[...]) + b_out_ref[...]
    x = x_ref[0] + _rmsnorm(m, g_post_ref[...])
    out_ref[0] = _ffn_tail(x, (g2_ref, f_in_ref, f_out_ref, g3_ref))


def _conv_out(u, x, w_dw, b_dw, ln_g, ln_b, w_out, b_out, g_post, ffn):
    b, seq, d = x.shape
    tm = TOKEN_TILE
    halo = CONV_HALO
    r = tm // halo
    last = seq // halo - 1
    rows_w = CONV_STRIP + 2 * halo
    out_row = lax.broadcasted_iota(jnp.int32, ((SUBLANES - 1) * rows_w, rows_w), 0)
    in_row = lax.broadcasted_iota(jnp.int32, ((SUBLANES - 1) * rows_w, rows_w), 1)
    shift = (in_row == out_row % rows_w + out_row // rows_w + 1).astype(BF16)
    return pl.pallas_call(
        functools.partial(_conv_out_kernel, strip=CONV_STRIP),
        grid=(b, seq // tm),
        in_specs=[
            pl.BlockSpec((1, halo, d), lambda bi, i: (bi, jnp.maximum(i * r - 1, 0), 0)),
            pl.BlockSpec((1, tm, d), lambda bi, i: (bi, i, 0)),
            pl.BlockSpec((1, halo, d), lambda bi, i: (bi, jnp.minimum((i + 1) * r, last), 0)),
            pl.BlockSpec((1, tm, d), lambda bi, i: (bi, i, 0)),
            _resident(w_dw.shape),
            _resident((1, d)),
            _resident((1, d)),
            _resident((1, d)),
            _resident(w_out.shape),
            _resident((1, d)),
            _resident((1, d)),
            *ffn.specs(),
            _resident(shift.shape),
        ],
        out_specs=pl.BlockSpec((1, tm, d), lambda bi, i: (bi, i, 0)),
        out_shape=jax.ShapeDtypeStruct((b, seq, d), F32),
        scratch_shapes=[pltpu.VMEM((tm + 2 * halo, d), BF16)],
        compiler_params=_params("parallel", "parallel"),
        name="conv_out_ffn",
    )(u, u, u, x, w_dw, b_dw, ln_g, ln_b, w_out, b_out, g_post, *ffn, shift)


def _conv_layer(x, g_pre, g_post, w_in, b_in, w_dw, b_dw, ln_g, ln_b, w_out, b_out, ffn):
    b, seq, d = x.shape
    u = _conv_in(x.reshape(b * seq, d), g_pre, w_in, b_in).reshape(b, seq, d)
    return _conv_out(u, x, w_dw, b_dw, ln_g, ln_b, w_out, b_out, g_post, ffn)


def _attn_proj_kernel(x_ref, g_ref, w_ref, c_ref, s_lo_ref, s_hi_ref, *rest, dils, rot_width, chunk):
    o_refs = rest[:len(dils)]
    xs_ref = rest[len(dils)]
    xp_refs = rest[len(dils) + 1:]
    tm = x_ref.shape[0]
    group_width = w_ref.shape[1] // len(dils)
    half = ROT_DIM // 2
    xn = _rmsnorm(x_ref[...], g_ref[...])
    for cb in range(xs_ref.shape[0]):
        xs_ref[cb] = xn[:, cb * LANES:(cb + 1) * LANES]
    staged = 0
    for g, dil in enumerate(dils):
        n = tm // dil
        if dil == 1:
            xp = xn.astype(BF16)
        else:
            xp_ref = xp_refs[staged]
            staged += 1
            for r in range(dil):
                for cb in range(xs_ref.shape[0]):
                    xp_ref[r * n:(r + 1) * n, cb * LANES:(cb + 1) * LANES] = (
                        xs_ref[cb, pl.ds(r, n, stride=dil), :].astype(BF16))
            xp = xp_ref[...]

        def permuted(tbl_ref):
            if dil == 1:
                return tbl_ref[...]
            return jnp.concatenate([tbl_ref[pl.ds(r, n, stride=dil), :] for r in range(dil)], axis=0)

        cmul = permuted(c_ref)
        s_lo = permuted(s_lo_ref)
        s_hi = permuted(s_hi_ref)
        for lo in range(0, group_width, chunk):
            a = _dot(xp, w_ref[:, g * group_width + lo:g * group_width + lo + chunk])
            for sub in range(0, chunk, LANES):
                piece = a[:, sub:sub + LANES]
                if lo < rot_width:
                    piece = (piece * cmul + pltpu.roll(piece, LANES - half, 1) * s_lo
                             + pltpu.roll(piece, half, 1) * s_hi)
                piece = piece.astype(BF16)
                for r in range(dil):
                    o_refs[g][0, r, :, lo + sub:lo + sub + LANES] = piece[r * n:(r + 1) * n, :]


def _attn_proj(x2d, g, w, tables, dils, seq):
    t, d = x2d.shape
    width = w.shape[1] // len(dils)
    tm = TOKEN_TILE
    blocks_per_seq = seq // tm
    b = t // seq
    table_spec = pl.BlockSpec((tm, LANES), lambda i: (i % blocks_per_seq, 0))
    return pl.pallas_call(
        functools.partial(_attn_proj_kernel, dils=dils, rot_width=2 * ATTN_HEADS * ATTN_HEAD_DIM, chunk=512),
        grid=(t // tm,),
        in_specs=[
            pl.BlockSpec((tm, d), lambda i: (i, 0)),
            _resident((1, d)),
            _resident(w.shape),
            table_spec, table_spec, table_spec,
        ],
        out_specs=[pl.BlockSpec((1, dil, tm // dil, width),
                                lambda i: (i // blocks_per_seq, 0, i % blocks_per_seq, 0)) for dil in dils],
        out_shape=[jax.ShapeDtypeStruct((b, dil, seq // dil, width), BF16) for dil in dils],
        scratch_shapes=[pltpu.VMEM((d // LANES, tm, LANES), F32)]
        + [pltpu.VMEM((tm, d), BF16) for dil in dils if dil > 1],
        compiler_params=_params("parallel"),
        name="attn_proj",
    )(x2d, g, w, *tables)


def _band_attn_kernel(q_ref, kp_ref, kc_ref, kn_ref, vp_ref, vc_ref, vn_ref, o_ref, stat_ref, bias_ref, *,
                      heads, head_dim, radius, sub_len, tq):
    i = pl.program_id(1)
    rows_step = q_ref.shape[1]
    n_sub = rows_step // tq
    nk = tq + 2 * radius
    row = lax.broadcasted_iota(jnp.int32, (tq, nk), 0)
    col = lax.broadcasted_iota(jnp.int32, (tq, nk), 1)
    off = col - row
    band = (off >= 0) & (off <= 2 * radius)
    lane = lax.broadcasted_iota(jnp.int32, (tq, LANES), 1)
    first = lane < head_dim
    keep_first = first.astype(F32).astype(BF16)
    keep_second = (1.0 - first.astype(F32)).astype(BF16)
    ones = jnp.ones((nk, LANES), BF16)

    width = q_ref.shape[2]

    def window(prev_ref, cur_ref, next_ref, sq, r0):
        before = prev_ref[sq] if r0 == 0 else cur_ref[sq, r0 - radius:r0, :]
        after = next_ref[sq] if r0 + tq == rows_step else cur_ref[sq, r0 + tq:r0 + tq + radius, :]
        return jnp.concatenate([before, cur_ref[sq, r0:r0 + tq, :], after], axis=0)

    for sub in range(n_sub):
        r0 = sub * tq
        kpos = i * rows_step + r0 - radius + col
        bias_ref[sub] = jnp.where(band & (kpos >= 0) & (kpos < sub_len), 0.0, MASK_VALUE)
        for sq in range(q_ref.shape[0]):
            kwin = window(kp_ref, kc_ref, kn_ref, sq, r0)
            vwin = window(vp_ref, vc_ref, vn_ref, sq, r0)
            stat = jnp.zeros((tq, LANES), F32)
            for pair in range(heads // 2):
                cols = slice(pair * LANES, (pair + 1) * LANES)
                qp = q_ref[sq, r0:r0 + tq, cols]
                s2 = _dot_nt(jnp.concatenate([qp * keep_first, qp * keep_second], axis=0), kwin[:, cols])
                outs = []
                for k in range(2):
                    h = 2 * pair + k
                    s = s2[k * tq:(k + 1) * tq] + bias_ref[sub]
                    mx = jnp.max(s, axis=-1, keepdims=True)
                    p = jnp.exp(s - mx).astype(BF16)
                    pv = _dot(p, jnp.concatenate([vwin[:, cols], ones], axis=1))
                    outs.append(pv[:, :LANES])
                    stat = jnp.where(lane == h, mx, stat)
                    stat = jnp.where(lane == heads + h, pv[:, LANES:], stat)
                o_lo = sq * width + pair * LANES
                o_ref[0, r0:r0 + tq, o_lo:o_lo + LANES] = jnp.where(first, outs[0], outs[1]).astype(o_ref.dtype)
            stat_ref[0, r0:r0 + tq, sq * LANES:(sq + 1) * LANES] = stat


def _band_attn(qkv, dil, radius):
    assert 2 * ATTN_HEADS <= LANES and 2 * ATTN_HEAD_DIM == LANES
    b, _, sub_len, w3 = qkv.shape
    width = w3 // 3
    tq = ATTN_Q_BLOCK
    rows_step = min(ATTN_Q_STEP, sub_len)
    nstep = sub_len // rows_step
    assert radius <= tq and rows_step % tq == 0 and sub_len % rows_step == 0 and rows_step % radius == 0
    flat = qkv.reshape(b * dil, sub_len, w3)
    per_step = rows_step // radius
    last_halo = sub_len // radius - 1
    n_seq = max(1, min(dil, ATTN_Q_STEP // rows_step))
    assert dil % n_seq == 0
    per_row = dil // n_seq

    def body(col):
        return pl.BlockSpec((n_seq, rows_step, width), lambda s, i: (s, i, col))

    def before(col):
        return pl.BlockSpec((n_seq, radius, width), lambda s, i: (s, jnp.maximum(i * per_step - 1, 0), col))

    def after(col):
        return pl.BlockSpec((n_seq, radius, width), lambda s, i: (s, jnp.minimum((i + 1) * per_step, last_halo), col))

    return pl.pallas_call(
        functools.partial(_band_attn_kernel, heads=ATTN_HEADS, head_dim=ATTN_HEAD_DIM, radius=radius,
                          sub_len=sub_len, tq=tq),
        grid=(b * per_row, nstep),
        in_specs=[body(0), before(1), body(1), after(1), before(2), body(2), after(2)],
        out_specs=[
            pl.BlockSpec((1, rows_step, n_seq * width), lambda s, i: (s // per_row, i, s % per_row)),
            pl.BlockSpec((1, rows_step, n_seq * LANES), lambda s, i: (s // per_row, i, s % per_row)),
        ],
        out_shape=[
            jax.ShapeDtypeStruct((b, sub_len, dil * width), BF16),
            jax.ShapeDtypeStruct((b, sub_len, dil * LANES), F32),
        ],
        scratch_shapes=[pltpu.VMEM((rows_step // tq, tq, tq + 2 * radius), F32)],
        compiler_params=_params("parallel", "parallel"),
        name=f"band_attn_d{dil}",
    )(flat, flat, flat, flat, flat, flat, flat)


def _attn_out_kernel(*refs, dils, head_dim):
    n_groups = len(dils)
    o_refs = refs[:n_groups]
    lse_refs = refs[n_groups:2 * n_groups]
    x_ref, w_ref, g_post_ref, *ffn_refs, out_ref = refs[2 * n_groups:2 * n_groups + 8]
    scratch = refs[2 * n_groups + 8:]
    tm, d = x_ref.shape[1], x_ref.shape[2]

    def natural(ref, dil, width, buf):
        if dil == 1:
            return ref[0].astype(F32)
        n = tm // dil
        for r in range(dil):
            for cb in range(width // LANES):
                lo = r * width + cb * LANES
                buf[cb, pl.ds(r, n, stride=dil), :] = ref[0, :, lo:lo + LANES].astype(F32)
        return jnp.concatenate([buf[cb] for cb in range(width // LANES)], axis=1)

    outs, lses = [], []
    k = 0
    for g, dil in enumerate(dils):
        if dil == 1:
            outs.append(natural(o_refs[g], 1, d, None))
            lses.append(natural(lse_refs[g], 1, LANES, None))
        else:
            outs.append(natural(o_refs[g], dil, d, scratch[k]))
            lses.append(natural(lse_refs[g], dil, LANES, scratch[k + 1]))
            k += 2
    heads = d // head_dim
    head_lane = lax.broadcasted_iota(jnp.int32, (tm, LANES), 1) < heads
    mx = functools.reduce(jnp.maximum, lses)
    es = [jnp.exp(l - mx) for l in lses]
    dens = [pltpu.roll(l, LANES - heads, 1) for l in lses]
    tot = functools.reduce(lambda a, b_: a + b_, [e * dn for e, dn in zip(es, dens)])
    tot = jnp.where(head_lane, tot, 1.0)
    expand = (lax.broadcasted_iota(jnp.int32, (2 * LANES, d), 1) // head_dim
              == lax.broadcasted_iota(jnp.int32, (2 * LANES, d), 0) % LANES).astype(BF16)
    mixed = jnp.zeros((tm, d), F32)
    for e, o in zip(es, outs):
        wgt = jnp.where(head_lane, e / tot, 0.0)
        hi = wgt.astype(BF16)
        lo = (wgt - hi.astype(F32)).astype(BF16)
        mixed = mixed + _dot(jnp.concatenate([hi, lo], axis=1), expand) * o
    m = _dot(mixed.astype(BF16), w_ref[...])
    x = x_ref[0] + _rmsnorm(m, g_post_ref[...])
    out_ref[0] = _ffn_tail(x, ffn_refs)


def _attn_out(os_, lses, x, w_out, g_post, dils, ffn):
    b, seq, d = x.shape
    tm = TOKEN_TILE
    in_specs = []
    for dil in dils:
        in_specs.append(pl.BlockSpec((1, tm // dil, dil * d), lambda bi, i: (bi, i, 0)))
    for dil in dils:
        in_specs.append(pl.BlockSpec((1, tm // dil, dil * LANES), lambda bi, i: (bi, i, 0)))
    in_specs += [pl.BlockSpec((1, tm, d), lambda bi, i: (bi, i, 0)), _resident(w_out.shape), _resident((1, d)),
                 *ffn.specs()]
    scratch = []
    for dil in dils:
        if dil > 1:
            scratch += [pltpu.VMEM((d // LANES, tm, LANES), F32), pltpu.VMEM((1, tm, LANES), F32)]
    return pl.pallas_call(
        functools.partial(_attn_out_kernel, dils=dils, head_dim=ATTN_HEAD_DIM),
        grid=(b, seq // tm),
        in_specs=in_specs,
        out_specs=pl.BlockSpec((1, tm, d), lambda bi, i: (bi, i, 0)),
        out_shape=jax.ShapeDtypeStruct((b, seq, d), F32),
        scratch_shapes=scratch,
        compiler_params=_params("parallel", "parallel"),
        name="attn_out_ffn",
    )(*os_, *lses, x, w_out, g_post, *ffn)


def _rotary_tables(seq):
    inv = ROPE_THETA ** (-jnp.arange(0, ROT_DIM, 2, dtype=F32) / ROT_DIM)
    ang = jnp.arange(seq, dtype=F32)[:, None] * inv[None, :]
    cos, sin = jnp.cos(ang), jnp.sin(ang)
    half = ROT_DIM // 2
    rest = ATTN_HEAD_DIM - ROT_DIM
    ones = jnp.ones((seq, rest), F32)
    zeros_h = jnp.zeros((seq, half), F32)
    zeros_r = jnp.zeros((seq, rest), F32)
    per_head = (
        jnp.concatenate([cos, cos, ones], axis=1),
        jnp.concatenate([-sin, zeros_h, zeros_r], axis=1),
        jnp.concatenate([zeros_h, sin, zeros_r], axis=1),
    )
    return [jnp.tile(tbl, (1, LANES // ATTN_HEAD_DIM)) for tbl in per_head]


def _attention_layer(x, g_pre, g_post, w_in, w_out, ffn):
    b, seq, d = x.shape
    x2d = x.reshape(b * seq, d)
    dils = tuple(dil for _, dil in DILATION_GROUPS)
    os_, lses = [], []
    qkvs = _attn_proj(x2d, g_pre, w_in, _rotary_tables(seq), dils, seq)
    for qkv, (window, dil) in zip(qkvs, DILATION_GROUPS):
        o, lse = _band_attn(qkv, dil, window // (2 * dil))
        os_.append(o)
        lses.append(lse)
    return _attn_out(os_, lses, x, w_out, g_post, dils, ffn)


def _scaled_bf16(w, lo, hi, scale):
    col = jnp.arange(w.shape[-1])
    return (w * jnp.where((col >= lo) & (col < hi), scale, 1.0).astype(w.dtype)).astype(BF16)


def kernel(x, norm_w, ffn_w_in, ffn_w_out, ret_w_in, ret_log1m_decay, ret_w_out, conv_w_in, conv_b_in, conv_w_dw,
           conv_b_dw, conv_ln_g, conv_ln_b, conv_w_out, conv_b_out, attn_w_in, attn_w_out):
    depth = norm_w.shape[0]
    b, seq, d = x.shape
    n_mixers = 3
    qk_w = RET_HEADS * RET_QK_DIM
    attn_gw = 3 * ATTN_HEADS * ATTN_HEAD_DIM
    attn_col = jnp.arange(attn_w_in.shape[-1]) % attn_gw
    attn_scale = jnp.where(attn_col < ATTN_HEADS * ATTN_HEAD_DIM, ATTN_HEAD_DIM ** -0.5, 1.0).astype(F32)
    row = lambda v: v.reshape(1, -1)
    for i in range(depth):
        kind, j = i % n_mixers, i // n_mixers
        g_pre, g_post = row(norm_w[i, 0]), row(norm_w[i, 1])
        ffn = Ffn(row(norm_w[i, 2]), ffn_w_in[i].astype(BF16), ffn_w_out[i].astype(BF16), row(norm_w[i, 3]))
        if kind == 0:
            w_in = _scaled_bf16(ret_w_in[j], qk_w, 2 * qk_w, RET_QK_DIM ** -0.5)
            x = _retention_layer(x, g_pre, g_post, w_in, ret_log1m_decay[j], ret_w_out[j].astype(BF16), ffn)
        elif kind == 1:
            x = _conv_layer(x, g_pre, g_post, conv_w_in[j].astype(BF16), row(conv_b_in[j]), conv_w_dw[j],
                            row(conv_b_dw[j]), row(conv_ln_g[j]), row(conv_ln_b[j]), conv_w_out[j].astype(BF16),
                            row(conv_b_out[j]), ffn)
        else:
            w_in = (attn_w_in[j] * attn_scale).astype(BF16)
            x = _attention_layer(x, g_pre, g_post, w_in, attn_w_out[j].astype(BF16), ffn)
    return x
```

```python
import functools
from typing import NamedTuple

import jax
import jax.numpy as jnp
from jax import lax
from jax.experimental import pallas as pl
from jax.experimental.pallas import tpu as pltpu

F32 = jnp.float32
BF16 = jnp.bfloat16

RMS_EPS = 1e-6
LN_EPS = 1e-5
MASK_VALUE = -1e30

RET_HEADS = 4
RET_QK_DIM = 256
RET_V_DIM = 512
RET_ROPE_BASE = 10000.0
ATTN_HEADS = 16
ATTN_HEAD_DIM = 64
DILATION_GROUPS = ((128, 1), (512, 4), (2048, 16))
ROPE_THETA = 500000.0
ROT_DIM = ATTN_HEAD_DIM // 4

LANES = 128
SUBLANES = 8
V7X_VMEM_LIMIT_BYTES = 56 * 1024 * 1024

TOKEN_TILE = 512
RET_CHUNK = 256
ATTN_Q_BLOCK = 128
ATTN_Q_STEP = 512
CONV_HALO = 16
CONV_STRIP = 128
PROJ_COLS = 512


def _params(*semantics):
    return pltpu.CompilerParams(dimension_semantics=semantics, vmem_limit_bytes=V7X_VMEM_LIMIT_BYTES)


def _resident(shape):
    zeros = (0,) * len(shape)
    return pl.BlockSpec(shape, lambda *_: zeros, pipeline_mode=pl.Buffered(1))


def _rmsnorm(x, g):
    return x * lax.rsqrt(jnp.mean(x * x, axis=-1, keepdims=True) + RMS_EPS) * g


def _silu(x):
    return x * jax.nn.sigmoid(x)


def _dot(a, b):
    return jnp.dot(a, b, preferred_element_type=F32)


def _dot_nt(a, b):
    return lax.dot_general(a, b, (((1,), (1,)), ((), ())), preferred_element_type=F32)


def _dot_tn(a, b):
    return lax.dot_general(a, b, (((0,), (0,)), ((), ())), preferred_element_type=F32)


def _ffn_hidden_chunk(hidden):
    for cand in (512, 256, 128):
        if hidden % cand == 0:
            return cand
    raise ValueError(f"hidden width {hidden} is not a multiple of {LANES}")


class Ffn(NamedTuple):
    g_pre: jax.Array
    w_in: jax.Array
    w_out: jax.Array
    g_post: jax.Array

    def specs(self):
        return [_resident(a.shape) for a in self]


def _ffn_tail(x, ffn_refs):
    g_pre_ref, w_in_ref, w_out_ref, g_post_ref = ffn_refs
    hidden = w_out_ref.shape[0]
    chunk = _ffn_hidden_chunk(hidden)
    xn = _rmsnorm(x, g_pre_ref[...]).astype(BF16)
    acc = jnp.zeros(x.shape, F32)
    for lo in range(0, hidden, chunk):
        a = _dot(xn, w_in_ref[:, lo:lo + chunk])
        b = _dot(xn, w_in_ref[:, hidden + lo:hidden + lo + chunk])
        h = (_silu(a) * b).astype(BF16)
        acc = acc + _dot(h, w_out_ref[lo:lo + chunk, :])
    return x + _rmsnorm(acc, g_post_ref[...])


def _ret_proj_kernel(x_ref, g_ref, w_ref, cos_ref, sin_ref, o_ref, *, rot_width, head_dim, plain_chunk):
    xn = _rmsnorm(x_ref[...], g_ref[...]).astype(BF16)
    cos = cos_ref[...]
    sin = sin_ref[...]
    half = head_dim // 2
    for h in range(rot_width // head_dim):
        lo = h * head_dim
        a = _dot(xn, w_ref[:, lo:lo + head_dim])
        x1 = a[:, :half]
        x2 = a[:, half:]
        o_ref[:, lo:lo + half] = (x1 * cos - x2 * sin).astype(BF16)
        o_ref[:, lo + half:lo + head_dim] = (x2 * cos + x1 * sin).astype(BF16)
    width = w_ref.shape[1]
    for lo in range(rot_width, width, plain_chunk):
        o_ref[:, lo:lo + plain_chunk] = _dot(xn, w_ref[:, lo:lo + plain_chunk]).astype(BF16)


def _ret_proj(x2d, g, w, cos, sin, seq):
    t, d = x2d.shape
    width = w.shape[1]
    tm = TOKEN_TILE
    blocks_per_seq = seq // tm
    half = RET_QK_DIM // 2
    return pl.pallas_call(
        functools.partial(_ret_proj_kernel, rot_width=2 * RET_HEADS * RET_QK_DIM, head_dim=RET_QK_DIM,
                          plain_chunk=PROJ_COLS),
        grid=(t // tm,),
        in_specs=[
            pl.BlockSpec((tm, d), lambda i: (i, 0)),
            _resident((1, d)),
            _resident(w.shape),
            pl.BlockSpec((tm, half), lambda i: (i % blocks_per_seq, 0)),
            pl.BlockSpec((tm, half), lambda i: (i % blocks_per_seq, 0)),
        ],
        out_specs=pl.BlockSpec((tm, width), lambda i: (i, 0)),
        out_shape=jax.ShapeDtypeStruct((t, width), BF16),
        compiler_params=_params("parallel"),
        name="ret_proj",
    )(x2d, g, w, cos, sin)


def _ret_core_kernel(decay_ref, q_ref, k_ref, v_ref, o_ref, acc_ref, stf_ref, stb_ref, *, chunk):
    head = pl.program_id(1)
    seq = q_ref.shape[1]
    n_chunks = seq // chunk
    c = chunk

    def log_gamma(direction):
        log1m = jnp.full((c, 1), decay_ref[direction, head], F32)
        return jnp.log(1.0 - jnp.exp(log1m))

    lgf = log_gamma(0)
    lgb = log_gamma(1)
    idx = lax.broadcasted_iota(jnp.int32, (c, 1), 0).astype(F32)
    xi_f = jnp.exp(lgf * (idx + 1.0))
    zeta_f = jnp.exp(lgf * (c - 1.0 - idx))
    decay_f = jnp.exp(lgf * float(c))
    xi_b = jnp.exp(lgb * (c - idx))
    zeta_b = jnp.exp(lgb * idx)
    decay_b = jnp.exp(lgb * float(c))
    diff = (lax.broadcasted_iota(jnp.int32, (c, c), 0) - lax.broadcasted_iota(jnp.int32, (c, c), 1)).astype(F32)
    dmat = jnp.exp(jnp.where(diff >= 0.0, lgf, -lgb) * diff)

    def rows(i):
        return pl.ds(pl.multiple_of(i * c, c), c)

    def state_update(st, decay, kc, zeta, vc):
        kz = (kc.astype(F32) * zeta).astype(BF16)
        return st * decay + _dot_tn(kz, vc)

    stf_ref[...] = jnp.zeros(stf_ref.shape, F32)
    stb_ref[...] = jnp.zeros(stb_ref.shape, F32)

    def parts(a, b):
        ra, rb = rows(a), rows(b)
        qa, ka, va = q_ref[0, ra, :], k_ref[0, ra, :], v_ref[0, ra, :]
        qb, kb, vb = q_ref[0, rb, :], k_ref[0, rb, :], v_ref[0, rb, :]
        stf = stf_ref[...]
        stb = stb_ref[...]
        scores = _dot_nt(qa, ka)
        cross_f = _dot(qa, stf.astype(BF16))
        cross_b = _dot(qb, stb.astype(BF16))
        stf_ref[...] = state_update(stf, decay_f, ka, zeta_f, va)
        stb_ref[...] = state_update(stb, decay_b, kb, zeta_b, vb)
        intra = _dot((scores * dmat).astype(BF16), va)
        return intra + xi_f * cross_f, xi_b * cross_b

    def finish(i, o):
        o = o * lax.rsqrt(jnp.mean(o * o, axis=-1, keepdims=True) + RMS_EPS)
        o_ref[0, rows(i), :] = o.astype(o_ref.dtype)

    half = n_chunks // 2

    def approach(j, carry):
        a, b = j, n_chunks - 1 - j
        part_a, part_b = parts(a, b)
        acc_ref[rows(a), :] = part_a
        acc_ref[rows(b), :] = part_b
        return carry

    def cross(j, carry):
        a, b = half + j, half - 1 - j
        part_a, part_b = parts(a, b)
        finish(a, acc_ref[rows(a), :] + part_a)
        finish(b, acc_ref[rows(b), :] + part_b)
        return carry

    lax.fori_loop(0, half, approach, 0, unroll=True)
    lax.fori_loop(0, half, cross, 0, unroll=True)


def _ret_core(proj3d, log1m_decay):
    b, seq, _ = proj3d.shape
    dk, dv, heads = RET_QK_DIM, RET_V_DIM, RET_HEADS
    assert seq % (2 * RET_CHUNK) == 0
    k_blk0 = heads
    v_blk0 = 2 * heads * dk // dv
    return pl.pallas_call(
        functools.partial(_ret_core_kernel, chunk=RET_CHUNK),
        grid=(b, heads),
        in_specs=[
            pl.BlockSpec(memory_space=pltpu.SMEM),
            pl.BlockSpec((1, seq, dk), lambda bi, h: (bi, 0, h)),
            pl.BlockSpec((1, seq, dk), lambda bi, h: (bi, 0, k_blk0 + h)),
            pl.BlockSpec((1, seq, dv), lambda bi, h: (bi, 0, v_blk0 + h)),
        ],
        out_specs=pl.BlockSpec((1, seq, dv), lambda bi, h: (bi, 0, h)),
        out_shape=jax.ShapeDtypeStruct((b, seq, heads * dv), BF16),
        scratch_shapes=[pltpu.VMEM((seq, dv), F32), pltpu.VMEM((dk, dv), F32), pltpu.VMEM((dk, dv), F32)],
        compiler_params=_params("parallel", "parallel"),
        name="ret_core",
    )(log1m_decay, proj3d, proj3d, proj3d)


def _ret_out_kernel(o_ref, gate_ref, x_ref, w_ref, g_post_ref, *rest):
    *ffn_refs, out_ref = rest
    y = (_silu(gate_ref[...].astype(F32)) * o_ref[...].astype(F32)).astype(BF16)
    x = x_ref[...] + _rmsnorm(_dot(y, w_ref[...]), g_post_ref[...])
    out_ref[...] = _ffn_tail(x, ffn_refs)


def _ret_out(o2d, proj2d, x2d, w_out, g_post, ffn):
    t, d = x2d.shape
    vw = o2d.shape[1]
    gate_blk = proj2d.shape[1] // vw - 1
    tm = TOKEN_TILE
    return pl.pallas_call(
        _ret_out_kernel,
        grid=(t // tm,),
        in_specs=[
            pl.BlockSpec((tm, vw), lambda i: (i, 0)),
            pl.BlockSpec((tm, vw), lambda i: (i, gate_blk)),
            pl.BlockSpec((tm, d), lambda i: (i, 0)),
            _resident(w_out.shape),
            _resident((1, d)),
            *ffn.specs(),
        ],
        out_specs=pl.BlockSpec((tm, d), lambda i: (i, 0)),
        out_shape=jax.ShapeDtypeStruct((t, d), F32),
        compiler_params=_params("parallel"),
        name="ret_out_ffn",
    )(o2d, proj2d, x2d, w_out, g_post, *ffn)


def _retention_layer(x, g_pre, g_post, w_in, log1m_decay, w_out, ffn):
    b, seq, d = x.shape
    inv = 1.0 / (RET_ROPE_BASE ** jnp.linspace(0.0, 1.0, RET_QK_DIM // 2, dtype=F32))
    ang = jnp.arange(seq, dtype=F32)[:, None] * inv[None, :]
    x2d = x.reshape(b * seq, d)
    proj = _ret_proj(x2d, g_pre, w_in, jnp.cos(ang), jnp.sin(ang), seq)
    o = _ret_core(proj.reshape(b, seq, -1), log1m_decay)
    return _ret_out(o.reshape(b * seq, -1), proj, x2d, w_out, g_post, ffn).reshape(b, seq, d)


def _conv_in_kernel(x_ref, g_ref, w_ref, b_ref, u_ref, *, chunk):
    d = u_ref.shape[1]
    xn = _rmsnorm(x_ref[...], g_ref[...]).astype(BF16)
    for lo in range(0, d, chunk):
        a = _dot(xn, w_ref[:, lo:lo + chunk]) + b_ref[:, lo:lo + chunk]
        gate = _dot(xn, w_ref[:, d + lo:d + lo + chunk]) + b_ref[:, d + lo:d + lo + chunk]
        u_ref[:, lo:lo + chunk] = (a * jax.nn.sigmoid(gate)).astype(BF16)


def _conv_in(x2d, g, w_in, b_in):
    t, d = x2d.shape
    tm = TOKEN_TILE
    return pl.pallas_call(
        functools.partial(_conv_in_kernel, chunk=PROJ_COLS),
        grid=(t // tm,),
        in_specs=[
            pl.BlockSpec((tm, d), lambda i: (i, 0)),
            _resident((1, d)),
            _resident(w_in.shape),
            _resident(b_in.shape),
        ],
        out_specs=pl.BlockSpec((tm, d), lambda i: (i, 0)),
        out_shape=jax.ShapeDtypeStruct((t, d), BF16),
        compiler_params=_params("parallel"),
        name="conv_in",
    )(x2d, g, w_in, b_in)


def _conv_out_kernel(u_prev_ref, u_ref, u_next_ref, x_ref, w_dw_ref, b_dw_ref, ln_g_ref, ln_b_ref,
                     w_ref, b_out_ref, g_post_ref, g2_ref, f_in_ref, f_out_ref, g3_ref, shift_ref, out_ref, ext_ref, *,
                     strip):
    i = pl.program_id(1)
    tm = u_ref.shape[1]
    halo = u_prev_ref.shape[1]
    taps = w_dw_ref.shape[0]
    pad = (taps - 1) // 2
    zero = jnp.zeros((halo, u_ref.shape[2]), BF16)
    ext_ref[0:halo, :] = jnp.where(i > 0, u_prev_ref[0], zero)
    ext_ref[halo:halo + tm, :] = u_ref[0]
    ext_ref[halo + tm:, :] = jnp.where(i < pl.num_programs(1) - 1, u_next_ref[0], zero)
    rows_w = strip + 2 * halo
    strips = []
    for r0 in range(0, tm, strip):
        window = ext_ref[r0:r0 + rows_w, :]
        moved = _dot(shift_ref[...], window)
        aligned = window.astype(F32)
        acc = jnp.zeros((strip, u_ref.shape[2]), F32) + b_dw_ref[...]
        for res in range(SUBLANES):
            for k in range(taps):
                off = halo - pad + k
                if off % SUBLANES == res:
                    if res == 0:
                        src = aligned[off:off + strip, :]
                    else:
                        lo = (res - 1) * rows_w + off - res
                        src = moved[lo:lo + strip, :]
                    acc = acc + src * w_dw_ref[k:k + 1, :]
        strips.append(acc)
    acc = jnp.concatenate(strips, axis=0)
    mu = jnp.mean(acc, axis=-1, keepdims=True)
    cen = acc - mu
    var = jnp.mean(cen * cen, axis=-1, keepdims=True)
    y = cen * lax.rsqrt(var + LN_EPS) * ln_g_ref[...] + ln_b_ref[...]
    m = _dot(_silu(y).astype(BF16), w_ref[...]) + b_out_ref[...]
    x = x_ref[0] + _rmsnorm(m, g_post_ref[...])
    out_ref[0] = _ffn_tail(x, (g2_ref, f_in_ref, f_out_ref, g3_ref))


def _conv_out(u, x, w_dw, b_dw, ln_g, ln_b, w_out, b_out, g_post, ffn):
    b, seq, d = x.shape
    tm = TOKEN_TILE
    halo = CONV_HALO
    r = tm // halo
    last = seq // halo - 1
    assert halo >= (w_dw.shape[0] - 1) // 2 and tm % CONV_STRIP == 0
    rows_w = CONV_STRIP + 2 * halo
    out_row = lax.broadcasted_iota(jnp.int32, ((SUBLANES - 1) * rows_w, rows_w), 0)
    in_row = lax.broadcasted_iota(jnp.int32, ((SUBLANES - 1) * rows_w, rows_w), 1)
    shift = (in_row == out_row % rows_w + out_row // rows_w + 1).astype(BF16)
    return pl.pallas_call(
        functools.partial(_conv_out_kernel, strip=CONV_STRIP),
        grid=(b, seq // tm),
        in_specs=[
            pl.BlockSpec((1, halo, d), lambda bi, i: (bi, jnp.maximum(i * r - 1, 0), 0)),
            pl.BlockSpec((1, tm, d), lambda bi, i: (bi, i, 0)),
            pl.BlockSpec((1, halo, d), lambda bi, i: (bi, jnp.minimum((i + 1) * r, last), 0)),
            pl.BlockSpec((1, tm, d), lambda bi, i: (bi, i, 0)),
            _resident(w_dw.shape),
            _resident((1, d)),
            _resident((1, d)),
            _resident((1, d)),
            _resident(w_out.shape),
            _resident((1, d)),
            _resident((1, d)),
            *ffn.specs(),
            _resident(shift.shape),
        ],
        out_specs=pl.BlockSpec((1, tm, d), lambda bi, i: (bi, i, 0)),
        out_shape=jax.ShapeDtypeStruct((b, seq, d), F32),
        scratch_shapes=[pltpu.VMEM((tm + 2 * halo, d), BF16)],
        compiler_params=_params("parallel", "parallel"),
        name="conv_out_ffn",
    )(u, u, u, x, w_dw, b_dw, ln_g, ln_b, w_out, b_out, g_post, *ffn, shift)


def _conv_layer(x, g_pre, g_post, w_in, b_in, w_dw, b_dw, ln_g, ln_b, w_out, b_out, ffn):
    b, seq, d = x.shape
    u = _conv_in(x.reshape(b * seq, d), g_pre, w_in, b_in).reshape(b, seq, d)
    return _conv_out(u, x, w_dw, b_dw, ln_g, ln_b, w_out, b_out, g_post, ffn)


def _attn_proj_kernel(x_ref, g_ref, w_ref, c_ref, s_lo_ref, s_hi_ref, *rest, dils, rot_width, chunk):
    o_refs = rest[:len(dils)]
    xs_ref = rest[len(dils)]
    xp_refs = rest[len(dils) + 1:]
    tm = x_ref.shape[0]
    group_width = w_ref.shape[1] // len(dils)
    half = ROT_DIM // 2
    xn = _rmsnorm(x_ref[...], g_ref[...])
    for cb in range(xs_ref.shape[0]):
        xs_ref[cb] = xn[:, cb * LANES:(cb + 1) * LANES]
    staged = 0
    for g, dil in enumerate(dils):
        n = tm // dil
        if dil == 1:
            xp = xn.astype(BF16)
        else:
            xp_ref = xp_refs[staged]
            staged += 1
            for r in range(dil):
                for cb in range(xs_ref.shape[0]):
                    xp_ref[r * n:(r + 1) * n, cb * LANES:(cb + 1) * LANES] = (
                        xs_ref[cb, pl.ds(r, n, stride=dil), :].astype(BF16))
            xp = xp_ref[...]

        def permuted(tbl_ref):
            if dil == 1:
                return tbl_ref[...]
            return jnp.concatenate([tbl_ref[pl.ds(r, n, stride=dil), :] for r in range(dil)], axis=0)

        cmul = permuted(c_ref)
        s_lo = permuted(s_lo_ref)
        s_hi = permuted(s_hi_ref)
        for lo in range(0, group_width, chunk):
            a = _dot(xp, w_ref[:, g * group_width + lo:g * group_width + lo + chunk])
            for sub in range(0, chunk, LANES):
                piece = a[:, sub:sub + LANES]
                if lo < rot_width:
                    piece = (piece * cmul + pltpu.roll(piece, LANES - half, 1) * s_lo
                             + pltpu.roll(piece, half, 1) * s_hi)
                piece = piece.astype(BF16)
                for r in range(dil):
                    o_refs[g][0, r, :, lo + sub:lo + sub + LANES] = piece[r * n:(r + 1) * n, :]


def _attn_proj(x2d, g, w, tables, dils, seq):
    t, d = x2d.shape
    width = w.shape[1] // len(dils)
    tm = TOKEN_TILE
    blocks_per_seq = seq // tm
    b = t // seq
    table_spec = pl.BlockSpec((tm, LANES), lambda i: (i % blocks_per_seq, 0))
    return pl.pallas_call(
        functools.partial(_attn_proj_kernel, dils=dils, rot_width=2 * ATTN_HEADS * ATTN_HEAD_DIM, chunk=PROJ_COLS),
        grid=(t // tm,),
        in_specs=[
            pl.BlockSpec((tm, d), lambda i: (i, 0)),
            _resident((1, d)),
            _resident(w.shape),
            table_spec, table_spec, table_spec,
        ],
        out_specs=[pl.BlockSpec((1, dil, tm // dil, width),
                                lambda i: (i // blocks_per_seq, 0, i % blocks_per_seq, 0)) for dil in dils],
        out_shape=[jax.ShapeDtypeStruct((b, dil, seq // dil, width), BF16) for dil in dils],
        scratch_shapes=[pltpu.VMEM((d // LANES, tm, LANES), F32)]
        + [pltpu.VMEM((tm, d), BF16) for dil in dils if dil > 1],
        compiler_params=_params("parallel"),
        name="attn_proj",
    )(x2d, g, w, *tables)


def _band_attn_kernel(q_ref, kp_ref, kc_ref, kn_ref, vp_ref, vc_ref, vn_ref, o_ref, stat_ref, bias_ref, *,
                      heads, head_dim, radius, sub_len, tq):
    i = pl.program_id(1)
    rows_step = q_ref.shape[1]
    n_sub = rows_step // tq
    nk = tq + 2 * radius
    row = lax.broadcasted_iota(jnp.int32, (tq, nk), 0)
    col = lax.broadcasted_iota(jnp.int32, (tq, nk), 1)
    off = col - row
    band = (off >= 0) & (off <= 2 * radius)
    lane = lax.broadcasted_iota(jnp.int32, (tq, LANES), 1)
    first = lane < head_dim
    keep_first = first.astype(F32).astype(BF16)
    keep_second = (1.0 - first.astype(F32)).astype(BF16)
    ones = jnp.ones((nk, LANES), BF16)

    width = q_ref.shape[2]

    def window(prev_ref, cur_ref, next_ref, sq, r0):
        before = prev_ref[sq] if r0 == 0 else cur_ref[sq, r0 - radius:r0, :]
        after = next_ref[sq] if r0 + tq == rows_step else cur_ref[sq, r0 + tq:r0 + tq + radius, :]
        return jnp.concatenate([before, cur_ref[sq, r0:r0 + tq, :], after], axis=0)

    for sub in range(n_sub):
        r0 = sub * tq
        kpos = i * rows_step + r0 - radius + col
        bias_ref[sub] = jnp.where(band & (kpos >= 0) & (kpos < sub_len), 0.0, MASK_VALUE)
        for sq in range(q_ref.shape[0]):
            kwin = window(kp_ref, kc_ref, kn_ref, sq, r0)
            vwin = window(vp_ref, vc_ref, vn_ref, sq, r0)
            stat = jnp.zeros((tq, LANES), F32)
            for pair in range(heads // 2):
                cols = slice(pair * LANES, (pair + 1) * LANES)
                qp = q_ref[sq, r0:r0 + tq, cols]
                s2 = _dot_nt(jnp.concatenate([qp * keep_first, qp * keep_second], axis=0), kwin[:, cols])
                outs = []
                for k in range(2):
                    h = 2 * pair + k
                    s = s2[k * tq:(k + 1) * tq] + bias_ref[sub]
                    mx = jnp.max(s, axis=-1, keepdims=True)
                    p = jnp.exp(s - mx).astype(BF16)
                    pv = _dot(p, jnp.concatenate([vwin[:, cols], ones], axis=1))
                    outs.append(pv[:, :LANES])
                    stat = jnp.where(lane == h, mx, stat)
                    stat = jnp.where(lane == heads + h, pv[:, LANES:], stat)
                o_lo = sq * width + pair * LANES
                o_ref[0, r0:r0 + tq, o_lo:o_lo + LANES] = jnp.where(first, outs[0], outs[1]).astype(o_ref.dtype)
            stat_ref[0, r0:r0 + tq, sq * LANES:(sq + 1) * LANES] = stat


def _band_attn(qkv, dil, radius):
    assert 2 * ATTN_HEADS <= LANES and 2 * ATTN_HEAD_DIM == LANES
    b, _, sub_len, w3 = qkv.shape
    width = w3 // 3
    tq = ATTN_Q_BLOCK
    rows_step = min(ATTN_Q_STEP, sub_len)
    nstep = sub_len // rows_step
    assert radius <= tq and rows_step % tq == 0 and sub_len % rows_step == 0 and rows_step % radius == 0
    flat = qkv.reshape(b * dil, sub_len, w3)
    per_step = rows_step // radius
    last_halo = sub_len // radius - 1
    n_seq = max(1, min(dil, ATTN_Q_STEP // rows_step))
    assert dil % n_seq == 0
    per_row = dil // n_seq

    def body(col):
        return pl.BlockSpec((n_seq, rows_step, width), lambda s, i: (s, i, col))

    def before(col):
        return pl.BlockSpec((n_seq, radius, width), lambda s, i: (s, jnp.maximum(i * per_step - 1, 0), col))

    def after(col):
        return pl.BlockSpec((n_seq, radius, width), lambda s, i: (s, jnp.minimum((i + 1) * per_step, last_halo), col))

    return pl.pallas_call(
        functools.partial(_band_attn_kernel, heads=ATTN_HEADS, head_dim=ATTN_HEAD_DIM, radius=radius,
                          sub_len=sub_len, tq=tq),
        grid=(b * per_row, nstep),
        in_specs=[body(0), before(1), body(1), after(1), before(2), body(2), after(2)],
        out_specs=[
            pl.BlockSpec((1, rows_step, n_seq * width), lambda s, i: (s // per_row, i, s % per_row)),
            pl.BlockSpec((1, rows_step, n_seq * LANES), lambda s, i: (s // per_row, i, s % per_row)),
        ],
        out_shape=[
            jax.ShapeDtypeStruct((b, sub_len, dil * width), BF16),
            jax.ShapeDtypeStruct((b, sub_len, dil * LANES), F32),
        ],
        scratch_shapes=[pltpu.VMEM((rows_step // tq, tq, tq + 2 * radius), F32)],
        compiler_params=_params("parallel", "parallel"),
        name=f"band_attn_d{dil}",
    )(flat, flat, flat, flat, flat, flat, flat)


def _attn_out_kernel(*refs, dils, head_dim):
    n_groups = len(dils)
    o_refs = refs[:n_groups]
    stat_refs = refs[n_groups:2 * n_groups]
    x_ref, w_ref, g_post_ref, *ffn_refs, out_ref = refs[2 * n_groups:2 * n_groups + 8]
    scratch = refs[2 * n_groups + 8:]
    tm, d = x_ref.shape[1], x_ref.shape[2]

    def natural(ref, dil, width, buf):
        if dil == 1:
            return ref[0].astype(F32)
        n = tm // dil
        for r in range(dil):
            for cb in range(width // LANES):
                lo = r * width + cb * LANES
                buf[cb, pl.ds(r, n, stride=dil), :] = ref[0, :, lo:lo + LANES].astype(F32)
        return jnp.concatenate([buf[cb] for cb in range(width // LANES)], axis=1)

    outs, stats = [], []
    k = 0
    for g, dil in enumerate(dils):
        if dil == 1:
            outs.append(natural(o_refs[g], 1, d, None))
            stats.append(natural(stat_refs[g], 1, LANES, None))
        else:
            outs.append(natural(o_refs[g], dil, d, scratch[k]))
            stats.append(natural(stat_refs[g], dil, LANES, scratch[k + 1]))
            k += 2
    heads = d // head_dim
    head_lane = lax.broadcasted_iota(jnp.int32, (tm, LANES), 1) < heads
    mx = functools.reduce(jnp.maximum, stats)
    es = [jnp.exp(l - mx) for l in stats]
    dens = [pltpu.roll(l, LANES - heads, 1) for l in stats]
    tot = functools.reduce(lambda a, b_: a + b_, [e * dn for e, dn in zip(es, dens)])
    tot = jnp.where(head_lane, tot, 1.0)
    expand = (lax.broadcasted_iota(jnp.int32, (2 * LANES, d), 1) // head_dim
              == lax.broadcasted_iota(jnp.int32, (2 * LANES, d), 0) % LANES).astype(BF16)
    mixed = jnp.zeros((tm, d), F32)
    for e, o in zip(es, outs):
        wgt = jnp.where(head_lane, e / tot, 0.0)
        hi = wgt.astype(BF16)
        lo = (wgt - hi.astype(F32)).astype(BF16)
        mixed = mixed + _dot(jnp.concatenate([hi, lo], axis=1), expand) * o
    m = _dot(mixed.astype(BF16), w_ref[...])
    x = x_ref[0] + _rmsnorm(m, g_post_ref[...])
    out_ref[0] = _ffn_tail(x, ffn_refs)


def _attn_out(os_, stats, x, w_out, g_post, dils, ffn):
    b, seq, d = x.shape
    tm = TOKEN_TILE
    in_specs = []
    for dil in dils:
        in_specs.append(pl.BlockSpec((1, tm // dil, dil * d), lambda bi, i: (bi, i, 0)))
    for dil in dils:
        in_specs.append(pl.BlockSpec((1, tm // dil, dil * LANES), lambda bi, i: (bi, i, 0)))
    in_specs += [pl.BlockSpec((1, tm, d), lambda bi, i: (bi, i, 0)), _resident(w_out.shape), _resident((1, d)),
                 *ffn.specs()]
    scratch = []
    for dil in dils:
        if dil > 1:
            scratch += [pltpu.VMEM((d // LANES, tm, LANES), F32), pltpu.VMEM((1, tm, LANES), F32)]
    return pl.pallas_call(
        functools.partial(_attn_out_kernel, dils=dils, head_dim=ATTN_HEAD_DIM),
        grid=(b, seq // tm),
        in_specs=in_specs,
        out_specs=pl.BlockSpec((1, tm, d), lambda bi, i: (bi, i, 0)),
        out_shape=jax.ShapeDtypeStruct((b, seq, d), F32),
        scratch_shapes=scratch,
        compiler_params=_params("parallel", "parallel"),
        name="attn_out_ffn",
    )(*os_, *stats, x, w_out, g_post, *ffn)


def _rotary_tables(seq):
    inv = ROPE_THETA ** (-jnp.arange(0, ROT_DIM, 2, dtype=F32) / ROT_DIM)
    ang = jnp.arange(seq, dtype=F32)[:, None] * inv[None, :]
    cos, sin = jnp.cos(ang), jnp.sin(ang)
    half = ROT_DIM // 2
    rest = ATTN_HEAD_DIM - ROT_DIM
    ones = jnp.ones((seq, rest), F32)
    zeros_h = jnp.zeros((seq, half), F32)
    zeros_r = jnp.zeros((seq, rest), F32)
    per_head = (
        jnp.concatenate([cos, cos, ones], axis=1),
        jnp.concatenate([-sin, zeros_h, zeros_r], axis=1),
        jnp.concatenate([zeros_h, sin, zeros_r], axis=1),
    )
    return [jnp.tile(tbl, (1, LANES // ATTN_HEAD_DIM)) for tbl in per_head]


def _attention_layer(x, g_pre, g_post, w_in, w_out, ffn):
    b, seq, d = x.shape
    x2d = x.reshape(b * seq, d)
    dils = tuple(dil for _, dil in DILATION_GROUPS)
    os_, stats = [], []
    qkvs = _attn_proj(x2d, g_pre, w_in, _rotary_tables(seq), dils, seq)
    for qkv, (window, dil) in zip(qkvs, DILATION_GROUPS):
        o, stat = _band_attn(qkv, dil, window // (2 * dil))
        os_.append(o)
        stats.append(stat)
    return _attn_out(os_, stats, x, w_out, g_post, dils, ffn)


def _scaled_bf16(w, lo, hi, scale):
    col = jnp.arange(w.shape[-1])
    return (w * jnp.where((col >= lo) & (col < hi), scale, 1.0).astype(w.dtype)).astype(BF16)


def kernel(x, norm_w, ffn_w_in, ffn_w_out, ret_w_in, ret_log1m_decay, ret_w_out, conv_w_in, conv_b_in, conv_w_dw,
           conv_b_dw, conv_ln_g, conv_ln_b, conv_w_out, conv_b_out, attn_w_in, attn_w_out):
    depth = norm_w.shape[0]
    n_mixers = 3
    qk_w = RET_HEADS * RET_QK_DIM
    attn_gw = 3 * ATTN_HEADS * ATTN_HEAD_DIM
    attn_col = jnp.arange(attn_w_in.shape[-1]) % attn_gw
    attn_scale = jnp.where(attn_col < ATTN_HEADS * ATTN_HEAD_DIM, ATTN_HEAD_DIM ** -0.5, 1.0).astype(F32)
    row = lambda v: v.reshape(1, -1)
    for i in range(depth):
        kind, j = i % n_mixers, i // n_mixers
        g_pre, g_post = row(norm_w[i, 0]), row(norm_w[i, 1])
        ffn = Ffn(row(norm_w[i, 2]), ffn_w_in[i].astype(BF16), ffn_w_out[i].astype(BF16), row(norm_w[i, 3]))
        if kind == 0:
            w_in = _scaled_bf16(ret_w_in[j], qk_w, 2 * qk_w, RET_QK_DIM ** -0.5)
            x = _retention_layer(x, g_pre, g_post, w_in, ret_log1m_decay[j], ret_w_out[j].astype(BF16), ffn)
        elif kind == 1:
            x = _conv_layer(x, g_pre, g_post, conv_w_in[j].astype(BF16), row(conv_b_in[j]), conv_w_dw[j],
                            row(conv_b_dw[j]), row(conv_ln_g[j]), row(conv_ln_b[j]), conv_w_out[j].astype(BF16),
                            row(conv_b_out[j]), ffn)
        else:
            w_in = (attn_w_in[j] * attn_scale).astype(BF16)
            x = _attention_layer(x, g_pre, g_post, w_in, attn_w_out[j].astype(BF16), ffn)
    return x
```

```python
import functools
from typing import NamedTuple

import jax
import jax.numpy as jnp
from jax import lax
from jax.experimental import pallas as pl
from jax.experimental.pallas import tpu as pltpu

F32 = jnp.float32
BF16 = jnp.bfloat16

RMS_EPS = 1e-6
LN_EPS = 1e-5
MASK_VALUE = -1e30
LOG2_E = 1.4426950408889634

RET_HEADS = 4
RET_QK_DIM = 256
RET_V_DIM = 512
RET_ROPE_BASE = 10000.0
ATTN_HEADS = 16
ATTN_HEAD_DIM = 64
DILATION_GROUPS = ((128, 1), (512, 4), (2048, 16))
ROPE_THETA = 500000.0
ROT_DIM = ATTN_HEAD_DIM // 4

LANES = 128
SUBLANES = 8
V7X_VMEM_LIMIT_BYTES = 56 * 1024 * 1024

TOKEN_TILE = 512
RET_CHUNK = 256
ATTN_Q_BLOCK = 128
ATTN_Q_STEP = 512
CONV_HALO = 16
CONV_STRIP = 128
PROJ_COLS = 512


def _params(*semantics):
    return pltpu.CompilerParams(dimension_semantics=semantics, vmem_limit_bytes=V7X_VMEM_LIMIT_BYTES)


def _resident(shape):
    zeros = (0,) * len(shape)
    return pl.BlockSpec(shape, lambda *_: zeros, pipeline_mode=pl.Buffered(1))


def _rmsnorm(x, g):
    return x * lax.rsqrt(jnp.mean(x * x, axis=-1, keepdims=True) + RMS_EPS) * g


def _silu(x):
    return x * jax.nn.sigmoid(x)


def _dot(a, b):
    return jnp.dot(a, b, preferred_element_type=F32)


def _dot_nt(a, b):
    return lax.dot_general(a, b, (((1,), (1,)), ((), ())), preferred_element_type=F32)


def _dot_tn(a, b):
    return lax.dot_general(a, b, (((0,), (0,)), ((), ())), preferred_element_type=F32)


def _ffn_hidden_chunk(hidden):
    for cand in (512, 256, 128):
        if hidden % cand == 0:
            return cand
    raise ValueError(f"hidden width {hidden} is not a multiple of {LANES}")


class Ffn(NamedTuple):
    g_pre: jax.Array
    w_in: jax.Array
    w_out: jax.Array
    g_post: jax.Array

    def specs(self):
        return [_resident(a.shape) for a in self]


def _ffn_tail(x, ffn_refs):
    g_pre_ref, w_in_ref, w_out_ref, g_post_ref = ffn_refs
    hidden = w_out_ref.shape[0]
    chunk = _ffn_hidden_chunk(hidden)
    xn = _rmsnorm(x, g_pre_ref[...]).astype(BF16)
    acc = jnp.zeros(x.shape, F32)
    for lo in range(0, hidden, chunk):
        a = _dot(xn, w_in_ref[:, lo:lo + chunk])
        b = _dot(xn, w_in_ref[:, hidden + lo:hidden + lo + chunk])
        h = (_silu(a) * b).astype(BF16)
        acc = acc + _dot(h, w_out_ref[lo:lo + chunk, :])
    return x + _rmsnorm(acc, g_post_ref[...])


def _ret_proj_kernel(x_ref, g_ref, w_ref, cos_ref, sin_ref, o_ref, *, rot_width, head_dim, plain_chunk):
    xn = _rmsnorm(x_ref[...], g_ref[...]).astype(BF16)
    cos = cos_ref[...]
    sin = sin_ref[...]
    half = head_dim // 2
    for h in range(rot_width // head_dim):
        lo = h * head_dim
        a = _dot(xn, w_ref[:, lo:lo + head_dim])
        x1 = a[:, :half]
        x2 = a[:, half:]
        o_ref[:, lo:lo + half] = (x1 * cos - x2 * sin).astype(BF16)
        o_ref[:, lo + half:lo + head_dim] = (x2 * cos + x1 * sin).astype(BF16)
    width = w_ref.shape[1]
    for lo in range(rot_width, width, plain_chunk):
        o_ref[:, lo:lo + plain_chunk] = _dot(xn, w_ref[:, lo:lo + plain_chunk]).astype(BF16)


def _ret_proj(x2d, g, w, cos, sin, seq):
    t, d = x2d.shape
    width = w.shape[1]
    tm = TOKEN_TILE
    blocks_per_seq = seq // tm
    half = RET_QK_DIM // 2
    return pl.pallas_call(
        functools.partial(_ret_proj_kernel, rot_width=2 * RET_HEADS * RET_QK_DIM, head_dim=RET_QK_DIM,
                          plain_chunk=PROJ_COLS),
        grid=(t // tm,),
        in_specs=[
            pl.BlockSpec((tm, d), lambda i: (i, 0)),
            _resident((1, d)),
            _resident(w.shape),
            pl.BlockSpec((tm, half), lambda i: (i % blocks_per_seq, 0)),
            pl.BlockSpec((tm, half), lambda i: (i % blocks_per_seq, 0)),
        ],
        out_specs=pl.BlockSpec((tm, width), lambda i: (i, 0)),
        out_shape=jax.ShapeDtypeStruct((t, width), BF16),
        compiler_params=_params("parallel"),
        name="ret_proj",
    )(x2d, g, w, cos, sin)


def _ret_core_kernel(decay_ref, q_ref, k_ref, v_ref, o_ref, acc_ref, stf_ref, stb_ref, *, chunk):
    head = pl.program_id(1)
    seq = q_ref.shape[1]
    n_chunks = seq // chunk
    c = chunk

    def log_gamma(direction):
        log1m = jnp.full((c, 1), decay_ref[direction, head], F32)
        return jnp.log(1.0 - jnp.exp(log1m))

    lgf = log_gamma(0)
    lgb = log_gamma(1)
    idx = lax.broadcasted_iota(jnp.int32, (c, 1), 0).astype(F32)
    xi_f = jnp.exp(lgf * (idx + 1.0))
    zeta_f = jnp.exp(lgf * (c - 1.0 - idx))
    decay_f = jnp.exp(lgf * float(c))
    xi_b = jnp.exp(lgb * (c - idx))
    zeta_b = jnp.exp(lgb * idx)
    decay_b = jnp.exp(lgb * float(c))
    diff = (lax.broadcasted_iota(jnp.int32, (c, c), 0) - lax.broadcasted_iota(jnp.int32, (c, c), 1)).astype(F32)
    dmat = jnp.exp(jnp.where(diff >= 0.0, lgf, -lgb) * diff)

    def rows(i):
        return pl.ds(pl.multiple_of(i * c, c), c)

    def state_update(st, decay, kc, zeta, vc):
        kz = (kc.astype(F32) * zeta).astype(BF16)
        return st * decay + _dot_tn(kz, vc)

    stf_ref[...] = jnp.zeros(stf_ref.shape, F32)
    stb_ref[...] = jnp.zeros(stb_ref.shape, F32)

    def parts(a, b):
        ra, rb = rows(a), rows(b)
        qa, ka, va = q_ref[0, ra, :], k_ref[0, ra, :], v_ref[0, ra, :]
        qb, kb, vb = q_ref[0, rb, :], k_ref[0, rb, :], v_ref[0, rb, :]
        stf = stf_ref[...]
        stb = stb_ref[...]
        scores = _dot_nt(qa, ka)
        cross_f = _dot(qa, stf.astype(BF16))
        cross_b = _dot(qb, stb.astype(BF16))
        stf_ref[...] = state_update(stf, decay_f, ka, zeta_f, va)
        stb_ref[...] = state_update(stb, decay_b, kb, zeta_b, vb)
        intra = _dot((scores * dmat).astype(BF16), va)
        return intra + xi_f * cross_f, xi_b * cross_b

    def finish(i, o):
        o = o * lax.rsqrt(jnp.mean(o * o, axis=-1, keepdims=True) + RMS_EPS)
        o_ref[0, rows(i), :] = o.astype(o_ref.dtype)

    half = n_chunks // 2

    def approach(j, carry):
        a, b = j, n_chunks - 1 - j
        part_a, part_b = parts(a, b)
        acc_ref[rows(a), :] = part_a
        acc_ref[rows(b), :] = part_b
        return carry

    def cross(j, carry):
        a, b = half + j, half - 1 - j
        part_a, part_b = parts(a, b)
        finish(a, acc_ref[rows(a), :] + part_a)
        finish(b, acc_ref[rows(b), :] + part_b)
        return carry

    lax.fori_loop(0, half, approach, 0, unroll=True)
    lax.fori_loop(0, half, cross, 0, unroll=True)


def _ret_core(proj3d, log1m_decay):
    b, seq, _ = proj3d.shape
    dk, dv, heads = RET_QK_DIM, RET_V_DIM, RET_HEADS
    assert seq % (2 * RET_CHUNK) == 0
    k_blk0 = heads
    v_blk0 = 2 * heads * dk // dv
    return pl.pallas_call(
        functools.partial(_ret_core_kernel, chunk=RET_CHUNK),
        grid=(b, heads),
        in_specs=[
            pl.BlockSpec(memory_space=pltpu.SMEM),
            pl.BlockSpec((1, seq, dk), lambda bi, h: (bi, 0, h)),
            pl.BlockSpec((1, seq, dk), lambda bi, h: (bi, 0, k_blk0 + h)),
            pl.BlockSpec((1, seq, dv), lambda bi, h: (bi, 0, v_blk0 + h)),
        ],
        out_specs=pl.BlockSpec((1, seq, dv), lambda bi, h: (bi, 0, h)),
        out_shape=jax.ShapeDtypeStruct((b, seq, heads * dv), BF16),
        scratch_shapes=[pltpu.VMEM((seq, dv), F32), pltpu.VMEM((dk, dv), F32), pltpu.VMEM((dk, dv), F32)],
        compiler_params=_params("parallel", "parallel"),
        name="ret_core",
    )(log1m_decay, proj3d, proj3d, proj3d)


def _ret_out_kernel(o_ref, gate_ref, x_ref, w_ref, g_post_ref, *rest):
    *ffn_refs, out_ref = rest
    y = (_silu(gate_ref[...].astype(F32)) * o_ref[...].astype(F32)).astype(BF16)
    x = x_ref[...] + _rmsnorm(_dot(y, w_ref[...]), g_post_ref[...])
    out_ref[...] = _ffn_tail(x, ffn_refs)


def _ret_out(o2d, proj2d, x2d, w_out, g_post, ffn):
    t, d = x2d.shape
    vw = o2d.shape[1]
    gate_blk = proj2d.shape[1] // vw - 1
    tm = TOKEN_TILE
    return pl.pallas_call(
        _ret_out_kernel,
        grid=(t // tm,),
        in_specs=[
            pl.BlockSpec((tm, vw), lambda i: (i, 0)),
            pl.BlockSpec((tm, vw), lambda i: (i, gate_blk)),
            pl.BlockSpec((tm, d), lambda i: (i, 0)),
            _resident(w_out.shape),
            _resident((1, d)),
            *ffn.specs(),
        ],
        out_specs=pl.BlockSpec((tm, d), lambda i: (i, 0)),
        out_shape=jax.ShapeDtypeStruct((t, d), F32),
        compiler_params=_params("parallel"),
        name="ret_out_ffn",
    )(o2d, proj2d, x2d, w_out, g_post, *ffn)


def _retention_layer(x, g_pre, g_post, w_in, log1m_decay, w_out, ffn):
    b, seq, d = x.shape
    inv = 1.0 / (RET_ROPE_BASE ** jnp.linspace(0.0, 1.0, RET_QK_DIM // 2, dtype=F32))
    ang = jnp.arange(seq, dtype=F32)[:, None] * inv[None, :]
    x2d = x.reshape(b * seq, d)
    proj = _ret_proj(x2d, g_pre, w_in, jnp.cos(ang), jnp.sin(ang), seq)
    o = _ret_core(proj.reshape(b, seq, -1), log1m_decay)
    return _ret_out(o.reshape(b * seq, -1), proj, x2d, w_out, g_post, ffn).reshape(b, seq, d)


def _conv_in_kernel(x_ref, g_ref, w_ref, b_ref, u_ref, *, chunk):
    d = u_ref.shape[1]
    xn = _rmsnorm(x_ref[...], g_ref[...]).astype(BF16)
    for lo in range(0, d, chunk):
        a = _dot(xn, w_ref[:, lo:lo + chunk]) + b_ref[:, lo:lo + chunk]
        gate = _dot(xn, w_ref[:, d + lo:d + lo + chunk]) + b_ref[:, d + lo:d + lo + chunk]
        u_ref[:, lo:lo + chunk] = (a * jax.nn.sigmoid(gate)).astype(BF16)


def _conv_in(x2d, g, w_in, b_in):
    t, d = x2d.shape
    tm = TOKEN_TILE
    return pl.pallas_call(
        functools.partial(_conv_in_kernel, chunk=PROJ_COLS),
        grid=(t // tm,),
        in_specs=[
            pl.BlockSpec((tm, d), lambda i: (i, 0)),
            _resident((1, d)),
            _resident(w_in.shape),
            _resident(b_in.shape),
        ],
        out_specs=pl.BlockSpec((tm, d), lambda i: (i, 0)),
        out_shape=jax.ShapeDtypeStruct((t, d), BF16),
        compiler_params=_params("parallel"),
        name="conv_in",
    )(x2d, g, w_in, b_in)


def _conv_out_kernel(u_prev_ref, u_ref, u_next_ref, x_ref, w_dw_ref, b_dw_ref, ln_g_ref, ln_b_ref,
                     w_ref, b_out_ref, g_post_ref, g2_ref, f_in_ref, f_out_ref, g3_ref, shift_ref, out_ref, ext_ref, *,
                     strip):
    i = pl.program_id(1)
    tm = u_ref.shape[1]
    halo = u_prev_ref.shape[1]
    taps = w_dw_ref.shape[0]
    pad = (taps - 1) // 2
    zero = jnp.zeros((halo, u_ref.shape[2]), BF16)
    ext_ref[0:halo, :] = jnp.where(i > 0, u_prev_ref[0], zero)
    ext_ref[halo:halo + tm, :] = u_ref[0]
    ext_ref[halo + tm:, :] = jnp.where(i < pl.num_programs(1) - 1, u_next_ref[0], zero)
    rows_w = strip + 2 * halo
    strips = []
    for r0 in range(0, tm, strip):
        window = ext_ref[r0:r0 + rows_w, :]
        moved = _dot(shift_ref[...], window)
        aligned = window.astype(F32)
        acc = jnp.zeros((strip, u_ref.shape[2]), F32) + b_dw_ref[...]
        for res in range(SUBLANES):
            for k in range(taps):
                off = halo - pad + k
                if off % SUBLANES == res:
                    if res == 0:
                        src = aligned[off:off + strip, :]
                    else:
                        lo = (res - 1) * rows_w + off - res
                        src = moved[lo:lo + strip, :]
                    acc = acc + src * w_dw_ref[k:k + 1, :]
        strips.append(acc)
    acc = jnp.concatenate(strips, axis=0)
    mu = jnp.mean(acc, axis=-1, keepdims=True)
    cen = acc - mu
    var = jnp.mean(cen * cen, axis=-1, keepdims=True)
    y = cen * lax.rsqrt(var + LN_EPS) * ln_g_ref[...] + ln_b_ref[...]
    m = _dot(_silu(y).astype(BF16), w_ref[...]) + b_out_ref[...]
    x = x_ref[0] + _rmsnorm(m, g_post_ref[...])
    out_ref[0] = _ffn_tail(x, (g2_ref, f_in_ref, f_out_ref, g3_ref))


def _conv_out(u, x, w_dw, b_dw, ln_g, ln_b, w_out, b_out, g_post, ffn):
    b, seq, d = x.shape
    tm = TOKEN_TILE
    halo = CONV_HALO
    r = tm // halo
    last = seq // halo - 1
    assert halo >= (w_dw.shape[0] - 1) // 2 and tm % CONV_STRIP == 0
    rows_w = CONV_STRIP + 2 * halo
    out_row = lax.broadcasted_iota(jnp.int32, ((SUBLANES - 1) * rows_w, rows_w), 0)
    in_row = lax.broadcasted_iota(jnp.int32, ((SUBLANES - 1) * rows_w, rows_w), 1)
    shift = (in_row == out_row % rows_w + out_row // rows_w + 1).astype(BF16)
    return pl.pallas_call(
        functools.partial(_conv_out_kernel, strip=CONV_STRIP),
        grid=(b, seq // tm),
        in_specs=[
            pl.BlockSpec((1, halo, d), lambda bi, i: (bi, jnp.maximum(i * r - 1, 0), 0)),
            pl.BlockSpec((1, tm, d), lambda bi, i: (bi, i, 0)),
            pl.BlockSpec((1, halo, d), lambda bi, i: (bi, jnp.minimum((i + 1) * r, last), 0)),
            pl.BlockSpec((1, tm, d), lambda bi, i: (bi, i, 0)),
            _resident(w_dw.shape),
            _resident((1, d)),
            _resident((1, d)),
            _resident((1, d)),
            _resident(w_out.shape),
            _resident((1, d)),
            _resident((1, d)),
            *ffn.specs(),
            _resident(shift.shape),
        ],
        out_specs=pl.BlockSpec((1, tm, d), lambda bi, i: (bi, i, 0)),
        out_shape=jax.ShapeDtypeStruct((b, seq, d), F32),
        scratch_shapes=[pltpu.VMEM((tm + 2 * halo, d), BF16)],
        compiler_params=_params("parallel", "parallel"),
        name="conv_out_ffn",
    )(u, u, u, x, w_dw, b_dw, ln_g, ln_b, w_out, b_out, g_post, *ffn, shift)


def _conv_layer(x, g_pre, g_post, w_in, b_in, w_dw, b_dw, ln_g, ln_b, w_out, b_out, ffn):
    b, seq, d = x.shape
    u = _conv_in(x.reshape(b * seq, d), g_pre, w_in, b_in).reshape(b, seq, d)
    return _conv_out(u, x, w_dw, b_dw, ln_g, ln_b, w_out, b_out, g_post, ffn)


def _attn_proj_kernel(x_ref, g_ref, w_ref, c_ref, s_lo_ref, s_hi_ref, *rest, dils, rot_width, chunk):
    o_refs = rest[:len(dils)]
    xs_ref = rest[len(dils)]
    xp_refs = rest[len(dils) + 1:]
    tm = x_ref.shape[0]
    group_width = w_ref.shape[1] // len(dils)
    half = ROT_DIM // 2
    xn = _rmsnorm(x_ref[...], g_ref[...])
    for cb in range(xs_ref.shape[0]):
        xs_ref[cb] = xn[:, cb * LANES:(cb + 1) * LANES]
    staged = 0
    for g, dil in enumerate(dils):
        n = tm // dil
        if dil == 1:
            xp = xn.astype(BF16)
        else:
            xp_ref = xp_refs[staged]
            staged += 1
            for r in range(dil):
                for cb in range(xs_ref.shape[0]):
                    xp_ref[r * n:(r + 1) * n, cb * LANES:(cb + 1) * LANES] = (
                        xs_ref[cb, pl.ds(r, n, stride=dil), :].astype(BF16))
            xp = xp_ref[...]

        def permuted(tbl_ref):
            if dil == 1:
                return tbl_ref[...]
            return jnp.concatenate([tbl_ref[pl.ds(r, n, stride=dil), :] for r in range(dil)], axis=0)

        cmul = permuted(c_ref)
        s_lo = permuted(s_lo_ref)
        s_hi = permuted(s_hi_ref)
        for lo in range(0, group_width, chunk):
            a = _dot(xp, w_ref[:, g * group_width + lo:g * group_width + lo + chunk])
            for sub in range(0, chunk, LANES):
                piece = a[:, sub:sub + LANES]
                if lo < rot_width:
                    piece = (piece * cmul + pltpu.roll(piece, LANES - half, 1) * s_lo
                             + pltpu.roll(piece, half, 1) * s_hi)
                piece = piece.astype(BF16)
                for r in range(dil):
                    o_refs[g][0, r, :, lo + sub:lo + sub + LANES] = piece[r * n:(r + 1) * n, :]


def _attn_proj(x2d, g, w, tables, dils, seq):
    t, d = x2d.shape
    width = w.shape[1] // len(dils)
    tm = TOKEN_TILE
    blocks_per_seq = seq // tm
    b = t // seq
    table_spec = pl.BlockSpec((tm, LANES), lambda i: (i % blocks_per_seq, 0))
    return pl.pallas_call(
        functools.partial(_attn_proj_kernel, dils=dils, rot_width=2 * ATTN_HEADS * ATTN_HEAD_DIM, chunk=PROJ_COLS),
        grid=(t // tm,),
        in_specs=[
            pl.BlockSpec((tm, d), lambda i: (i, 0)),
            _resident((1, d)),
            _resident(w.shape),
            table_spec, table_spec, table_spec,
        ],
        out_specs=[pl.BlockSpec((1, dil, tm // dil, width),
                                lambda i: (i // blocks_per_seq, 0, i % blocks_per_seq, 0)) for dil in dils],
        out_shape=[jax.ShapeDtypeStruct((b, dil, seq // dil, width), BF16) for dil in dils],
        scratch_shapes=[pltpu.VMEM((d // LANES, tm, LANES), F32)]
        + [pltpu.VMEM((tm, d), BF16) for dil in dils if dil > 1],
        compiler_params=_params("parallel"),
        name="attn_proj",
    )(x2d, g, w, *tables)


def _band_attn_kernel(q_ref, kp_ref, kc_ref, kn_ref, vp_ref, vc_ref, vn_ref, o_ref, stat_ref, bias_ref, *,
                      heads, head_dim, radius, sub_len, tq):
    i = pl.program_id(1)
    rows_step = q_ref.shape[1]
    n_sub = rows_step // tq
    nk = tq + 2 * radius
    row = lax.broadcasted_iota(jnp.int32, (tq, nk), 0)
    col = lax.broadcasted_iota(jnp.int32, (tq, nk), 1)
    off = col - row
    band = (off >= 0) & (off <= 2 * radius)
    lane = lax.broadcasted_iota(jnp.int32, (tq, LANES), 1)
    first = lane < head_dim
    keep_first = first.astype(F32).astype(BF16)
    keep_second = (1.0 - first.astype(F32)).astype(BF16)
    ones = jnp.ones((nk, LANES), BF16)

    width = q_ref.shape[2]

    def window(prev_ref, cur_ref, next_ref, sq, r0):
        before = prev_ref[sq] if r0 == 0 else cur_ref[sq, r0 - radius:r0, :]
        after = next_ref[sq] if r0 + tq == rows_step else cur_ref[sq, r0 + tq:r0 + tq + radius, :]
        return jnp.concatenate([before, cur_ref[sq, r0:r0 + tq, :], after], axis=0)

    for sub in range(n_sub):
        r0 = sub * tq
        kpos = i * rows_step + r0 - radius + col
        bias_ref[sub] = jnp.where(band & (kpos >= 0) & (kpos < sub_len), 0.0, MASK_VALUE)
        for sq in range(q_ref.shape[0]):
            kwin = window(kp_ref, kc_ref, kn_ref, sq, r0)
            vwin = window(vp_ref, vc_ref, vn_ref, sq, r0)
            stat = jnp.zeros((tq, LANES), F32)
            for pair in range(heads // 2):
                cols = slice(pair * LANES, (pair + 1) * LANES)
                qp = q_ref[sq, r0:r0 + tq, cols]
                s2 = _dot_nt(jnp.concatenate([qp * keep_first, qp * keep_second], axis=0), kwin[:, cols])
                outs = []
                for k in range(2):
                    h = 2 * pair + k
                    s = s2[k * tq:(k + 1) * tq] + bias_ref[sub]
                    mx = jnp.max(s, axis=-1, keepdims=True)
                    p = jnp.exp2(s - mx).astype(BF16)
                    pv = _dot(p, jnp.concatenate([vwin[:, cols], ones], axis=1))
                    outs.append(pv[:, :LANES])
                    stat = jnp.where(lane == h, mx, stat)
                    stat = jnp.where(lane == heads + h, pv[:, LANES:], stat)
                o_lo = sq * width + pair * LANES
                o_ref[0, r0:r0 + tq, o_lo:o_lo + LANES] = jnp.where(first, outs[0], outs[1]).astype(o_ref.dtype)
            stat_ref[0, r0:r0 + tq, sq * LANES:(sq + 1) * LANES] = stat


def _band_attn(qkv, dil, radius):
    assert 2 * ATTN_HEADS <= LANES and 2 * ATTN_HEAD_DIM == LANES
    b, _, sub_len, w3 = qkv.shape
    width = w3 // 3
    tq = ATTN_Q_BLOCK
    rows_step = min(ATTN_Q_STEP, sub_len)
    nstep = sub_len // rows_step
    assert radius <= tq and rows_step % tq == 0 and sub_len % rows_step == 0 and rows_step % radius == 0
    flat = qkv.reshape(b * dil, sub_len, w3)
    per_step = rows_step // radius
    last_halo = sub_len // radius - 1
    n_seq = max(1, min(dil, ATTN_Q_STEP // rows_step))
    assert dil % n_seq == 0
    per_row = dil // n_seq

    def body(col):
        return pl.BlockSpec((n_seq, rows_step, width), lambda s, i: (s, i, col))

    def before(col):
        return pl.BlockSpec((n_seq, radius, width), lambda s, i: (s, jnp.maximum(i * per_step - 1, 0), col))

    def after(col):
        return pl.BlockSpec((n_seq, radius, width), lambda s, i: (s, jnp.minimum((i + 1) * per_step, last_halo), col))

    return pl.pallas_call(
        functools.partial(_band_attn_kernel, heads=ATTN_HEADS, head_dim=ATTN_HEAD_DIM, radius=radius,
                          sub_len=sub_len, tq=tq),
        grid=(b * per_row, nstep),
        in_specs=[body(0), before(1), body(1), after(1), before(2), body(2), after(2)],
        out_specs=[
            pl.BlockSpec((1, rows_step, n_seq * width), lambda s, i: (s // per_row, i, s % per_row)),
            pl.BlockSpec((1, rows_step, n_seq * LANES), lambda s, i: (s // per_row, i, s % per_row)),
        ],
        out_shape=[
            jax.ShapeDtypeStruct((b, sub_len, dil * width), BF16),
            jax.ShapeDtypeStruct((b, sub_len, dil * LANES), F32),
        ],
        scratch_shapes=[pltpu.VMEM((rows_step // tq, tq, tq + 2 * radius), F32)],
        compiler_params=_params("parallel", "parallel"),
        name=f"band_attn_d{dil}",
    )(flat, flat, flat, flat, flat, flat, flat)


def _attn_out_kernel(*refs, dils, head_dim):
    n_groups = len(dils)
    o_refs = refs[:n_groups]
    stat_refs = refs[n_groups:2 * n_groups]
    x_ref, w_ref, g_post_ref, *ffn_refs, out_ref = refs[2 * n_groups:2 * n_groups + 8]
    scratch = refs[2 * n_groups + 8:]
    tm, d = x_ref.shape[1], x_ref.shape[2]

    def natural(ref, dil, width, buf):
        if dil == 1:
            return ref[0].astype(F32)
        n = tm // dil
        for r in range(dil):
            for cb in range(width // LANES):
                lo = r * width + cb * LANES
                buf[cb, pl.ds(r, n, stride=dil), :] = ref[0, :, lo:lo + LANES].astype(F32)
        return jnp.concatenate([buf[cb] for cb in range(width // LANES)], axis=1)

    outs, stats = [], []
    k = 0
    for g, dil in enumerate(dils):
        if dil == 1:
            outs.append(natural(o_refs[g], 1, d, None))
            stats.append(natural(stat_refs[g], 1, LANES, None))
        else:
            outs.append(natural(o_refs[g], dil, d, scratch[k]))
            stats.append(natural(stat_refs[g], dil, LANES, scratch[k + 1]))
            k += 2
    heads = d // head_dim
    head_lane = lax.broadcasted_iota(jnp.int32, (tm, LANES), 1) < heads
    mx = functools.reduce(jnp.maximum, stats)
    es = [jnp.exp2(l - mx) for l in stats]
    dens = [pltpu.roll(l, LANES - heads, 1) for l in stats]
    tot = functools.reduce(lambda a, b_: a + b_, [e * dn for e, dn in zip(es, dens)])
    tot = jnp.where(head_lane, tot, 1.0)
    expand = (lax.broadcasted_iota(jnp.int32, (2 * LANES, d), 1) // head_dim
              == lax.broadcasted_iota(jnp.int32, (2 * LANES, d), 0) % LANES).astype(BF16)
    mixed = jnp.zeros((tm, d), F32)
    for e, o in zip(es, outs):
        wgt = jnp.where(head_lane, e / tot, 0.0)
        hi = wgt.astype(BF16)
        lo = (wgt - hi.astype(F32)).astype(BF16)
        mixed = mixed + _dot(jnp.concatenate([hi, lo], axis=1), expand) * o
    m = _dot(mixed.astype(BF16), w_ref[...])
    x = x_ref[0] + _rmsnorm(m, g_post_ref[...])
    out_ref[0] = _ffn_tail(x, ffn_refs)


def _attn_out(os_, stats, x, w_out, g_post, dils, ffn):
    b, seq, d = x.shape
    tm = TOKEN_TILE
    in_specs = []
    for dil in dils:
        in_specs.append(pl.BlockSpec((1, tm // dil, dil * d), lambda bi, i: (bi, i, 0)))
    for dil in dils:
        in_specs.append(pl.BlockSpec((1, tm // dil, dil * LANES), lambda bi, i: (bi, i, 0)))
    in_specs += [pl.BlockSpec((1, tm, d), lambda bi, i: (bi, i, 0)), _resident(w_out.shape), _resident((1, d)),
                 *ffn.specs()]
    scratch = []
    for dil in dils:
        if dil > 1:
            scratch += [pltpu.VMEM((d // LANES, tm, LANES), F32), pltpu.VMEM((1, tm, LANES), F32)]
    return pl.pallas_call(
        functools.partial(_attn_out_kernel, dils=dils, head_dim=ATTN_HEAD_DIM),
        grid=(b, seq // tm),
        in_specs=in_specs,
        out_specs=pl.BlockSpec((1, tm, d), lambda bi, i: (bi, i, 0)),
        out_shape=jax.ShapeDtypeStruct((b, seq, d), F32),
        scratch_shapes=scratch,
        compiler_params=_params("parallel", "parallel"),
        name="attn_out_ffn",
    )(*os_, *stats, x, w_out, g_post, *ffn)


def _rotary_tables(seq):
    inv = ROPE_THETA ** (-jnp.arange(0, ROT_DIM, 2, dtype=F32) / ROT_DIM)
    ang = jnp.arange(seq, dtype=F32)[:, None] * inv[None, :]
    cos, sin = jnp.cos(ang), jnp.sin(ang)
    half = ROT_DIM // 2
    rest = ATTN_HEAD_DIM - ROT_DIM
    ones = jnp.ones((seq, rest), F32)
    zeros_h = jnp.zeros((seq, half), F32)
    zeros_r = jnp.zeros((seq, rest), F32)
    per_head = (
        jnp.concatenate([cos, cos, ones], axis=1),
        jnp.concatenate([-sin, zeros_h, zeros_r], axis=1),
        jnp.concatenate([zeros_h, sin, zeros_r], axis=1),
    )
    return [jnp.tile(tbl, (1, LANES // ATTN_HEAD_DIM)) for tbl in per_head]


def _attention_layer(x, g_pre, g_post, w_in, w_out, ffn):
    b, seq, d = x.shape
    x2d = x.reshape(b * seq, d)
    dils = tuple(dil for _, dil in DILATION_GROUPS)
    os_, stats = [], []
    qkvs = _attn_proj(x2d, g_pre, w_in, _rotary_tables(seq), dils, seq)
    for qkv, (window, dil) in zip(qkvs, DILATION_GROUPS):
        o, stat = _band_attn(qkv, dil, window // (2 * dil))
        os_.append(o)
        stats.append(stat)
    return _attn_out(os_, stats, x, w_out, g_post, dils, ffn)


def _scaled_bf16(w, lo, hi, scale):
    col = jnp.arange(w.shape[-1])
    return (w * jnp.where((col >= lo) & (col < hi), scale, 1.0).astype(w.dtype)).astype(BF16)


def kernel(x, norm_w, ffn_w_in, ffn_w_out, ret_w_in, ret_log1m_decay, ret_w_out, conv_w_in, conv_b_in, conv_w_dw,
           conv_b_dw, conv_ln_g, conv_ln_b, conv_w_out, conv_b_out, attn_w_in, attn_w_out):
    depth = norm_w.shape[0]
    n_mixers = 3
    qk_w = RET_HEADS * RET_QK_DIM
    attn_gw = 3 * ATTN_HEADS * ATTN_HEAD_DIM
    attn_col = jnp.arange(attn_w_in.shape[-1]) % attn_gw
    attn_scale = jnp.where(attn_col < ATTN_HEADS * ATTN_HEAD_DIM, ATTN_HEAD_DIM ** -0.5 * LOG2_E, 1.0).astype(F32)
    row = lambda v: v.reshape(1, -1)
    for i in range(depth):
        kind, j = i % n_mixers, i // n_mixers
        g_pre, g_post = row(norm_w[i, 0]), row(norm_w[i, 1])
        ffn = Ffn(row(norm_w[i, 2]), ffn_w_in[i].astype(BF16), ffn_w_out[i].astype(BF16), row(norm_w[i, 3]))
        if kind == 0:
            w_in = _scaled_bf16(ret_w_in[j], qk_w, 2 * qk_w, RET_QK_DIM ** -0.5)
            x = _retention_layer(x, g_pre, g_post, w_in, ret_log1m_decay[j], ret_w_out[j].astype(BF16), ffn)
        elif kind == 1:
            x = _conv_layer(x, g_pre, g_post, conv_w_in[j].astype(BF16), row(conv_b_in[j]), conv_w_dw[j],
                            row(conv_b_dw[j]), row(conv_ln_g[j]), row(conv_ln_b[j]), conv_w_out[j].astype(BF16),
                            row(conv_b_out[j]), ffn)
        else:
            w_in = (attn_w_in[j] * attn_scale).astype(BF16)
            x = _attention_layer(x, g_pre, g_post, w_in, attn_w_out[j].astype(BF16), ffn)
    return x
```

```python
import functools
from typing import NamedTuple

import jax
import jax.numpy as jnp
from jax import lax
from jax.experimental import pallas as pl
from jax.experimental.pallas import tpu as pltpu

F32 = jnp.float32
BF16 = jnp.bfloat16

RMS_EPS = 1e-6
LN_EPS = 1e-5
MASK_VALUE = -1e30
LOG2_E = 1.4426950408889634

RET_HEADS = 4
RET_QK_DIM = 256
RET_V_DIM = 512
RET_ROPE_BASE = 10000.0
ATTN_HEADS = 16
ATTN_HEAD_DIM = 64
DILATION_GROUPS = ((128, 1), (512, 4), (2048, 16))
ROPE_THETA = 500000.0
ROT_DIM = ATTN_HEAD_DIM // 4

LANES = 128
SUBLANES = 8
V7X_VMEM_LIMIT_BYTES = 56 * 1024 * 1024

TOKEN_TILE = 512
RET_CHUNK = 256
ATTN_Q_BLOCK = 128
ATTN_Q_STEP = 1024
CONV_HALO = 16
CONV_STRIP = 128
PROJ_COLS = 512


def _params(*semantics):
    return pltpu.CompilerParams(dimension_semantics=semantics, vmem_limit_bytes=V7X_VMEM_LIMIT_BYTES)


def _resident(shape):
    zeros = (0,) * len(shape)
    return pl.BlockSpec(shape, lambda *_: zeros, pipeline_mode=pl.Buffered(1))


def _rmsnorm(x, g):
    return x * lax.rsqrt(jnp.mean(x * x, axis=-1, keepdims=True) + RMS_EPS) * g


def _silu(x):
    return x * jax.nn.sigmoid(x)


def _dot(a, b):
    return jnp.dot(a, b, preferred_element_type=F32)


def _dot_nt(a, b):
    return lax.dot_general(a, b, (((1,), (1,)), ((), ())), preferred_element_type=F32)


def _dot_tn(a, b):
    return lax.dot_general(a, b, (((0,), (0,)), ((), ())), preferred_element_type=F32)


def _ffn_hidden_chunk(hidden):
    for cand in (512, 256, 128):
        if hidden % cand == 0:
            return cand
    raise ValueError(f"hidden width {hidden} is not a multiple of {LANES}")


class Ffn(NamedTuple):
    g_pre: jax.Array
    w_in: jax.Array
    w_out: jax.Array
    g_post: jax.Array

    def specs(self):
        return [_resident(a.shape) for a in self]


def _ffn_tail(x, ffn_refs):
    g_pre_ref, w_in_ref, w_out_ref, g_post_ref = ffn_refs
    hidden = w_out_ref.shape[0]
    chunk = _ffn_hidden_chunk(hidden)
    xn = _rmsnorm(x, g_pre_ref[...]).astype(BF16)
    acc = jnp.zeros(x.shape, F32)
    for lo in range(0, hidden, chunk):
        a = _dot(xn, w_in_ref[:, lo:lo + chunk])
        b = _dot(xn, w_in_ref[:, hidden + lo:hidden + lo + chunk])
        h = (_silu(a) * b).astype(BF16)
        acc = acc + _dot(h, w_out_ref[lo:lo + chunk, :])
    return x + _rmsnorm(acc, g_post_ref[...])


def _ret_proj_kernel(x_ref, g_ref, w_ref, cos_ref, sin_ref, o_ref, *, rot_width, head_dim, plain_chunk):
    xn = _rmsnorm(x_ref[...], g_ref[...]).astype(BF16)
    cos = cos_ref[...]
    sin = sin_ref[...]
    half = head_dim // 2
    for h in range(rot_width // head_dim):
        lo = h * head_dim
        a = _dot(xn, w_ref[:, lo:lo + head_dim])
        x1 = a[:, :half]
        x2 = a[:, half:]
        o_ref[:, lo:lo + half] = (x1 * cos - x2 * sin).astype(BF16)
        o_ref[:, lo + half:lo + head_dim] = (x2 * cos + x1 * sin).astype(BF16)
    width = w_ref.shape[1]
    for lo in range(rot_width, width, plain_chunk):
        o_ref[:, lo:lo + plain_chunk] = _dot(xn, w_ref[:, lo:lo + plain_chunk]).astype(BF16)


def _ret_proj(x2d, g, w, cos, sin, seq):
    t, d = x2d.shape
    width = w.shape[1]
    tm = TOKEN_TILE
    blocks_per_seq = seq // tm
    half = RET_QK_DIM // 2
    return pl.pallas_call(
        functools.partial(_ret_proj_kernel, rot_width=2 * RET_HEADS * RET_QK_DIM, head_dim=RET_QK_DIM,
                          plain_chunk=PROJ_COLS),
        grid=(t // tm,),
        in_specs=[
            pl.BlockSpec((tm, d), lambda i: (i, 0)),
            _resident((1, d)),
            _resident(w.shape),
            pl.BlockSpec((tm, half), lambda i: (i % blocks_per_seq, 0)),
            pl.BlockSpec((tm, half), lambda i: (i % blocks_per_seq, 0)),
        ],
        out_specs=pl.BlockSpec((tm, width), lambda i: (i, 0)),
        out_shape=jax.ShapeDtypeStruct((t, width), BF16),
        compiler_params=_params("parallel"),
        name="ret_proj",
    )(x2d, g, w, cos, sin)


def _ret_core_kernel(decay_ref, q_ref, k_ref, v_ref, o_ref, acc_ref, stf_ref, stb_ref, *, chunk):
    head = pl.program_id(1)
    seq = q_ref.shape[1]
    n_chunks = seq // chunk
    c = chunk

    def log_gamma(direction):
        log1m = jnp.full((c, 1), decay_ref[direction, head], F32)
        return jnp.log(1.0 - jnp.exp(log1m))

    lgf = log_gamma(0)
    lgb = log_gamma(1)
    idx = lax.broadcasted_iota(jnp.int32, (c, 1), 0).astype(F32)
    xi_f = jnp.exp(lgf * (idx + 1.0))
    zeta_f = jnp.exp(lgf * (c - 1.0 - idx))
    decay_f = jnp.exp(lgf * float(c))
    xi_b = jnp.exp(lgb * (c - idx))
    zeta_b = jnp.exp(lgb * idx)
    decay_b = jnp.exp(lgb * float(c))
    diff = (lax.broadcasted_iota(jnp.int32, (c, c), 0) - lax.broadcasted_iota(jnp.int32, (c, c), 1)).astype(F32)
    dmat = jnp.exp(jnp.where(diff >= 0.0, lgf, -lgb) * diff)

    def rows(i):
        return pl.ds(pl.multiple_of(i * c, c), c)

    def state_update(st, decay, kc, zeta, vc):
        kz = (kc.astype(F32) * zeta).astype(BF16)
        return st * decay + _dot_tn(kz, vc)

    stf_ref[...] = jnp.zeros(stf_ref.shape, F32)
    stb_ref[...] = jnp.zeros(stb_ref.shape, F32)

    def parts(a, b):
        ra, rb = rows(a), rows(b)
        qa, ka, va = q_ref[0, ra, :], k_ref[0, ra, :], v_ref[0, ra, :]
        qb, kb, vb = q_ref[0, rb, :], k_ref[0, rb, :], v_ref[0, rb, :]
        stf = stf_ref[...]
        stb = stb_ref[...]
        scores = _dot_nt(qa, ka)
        cross_f = _dot(qa, stf.astype(BF16))
        cross_b = _dot(qb, stb.astype(BF16))
        stf_ref[...] = state_update(stf, decay_f, ka, zeta_f, va)
        stb_ref[...] = state_update(stb, decay_b, kb, zeta_b, vb)
        intra = _dot((scores * dmat).astype(BF16), va)
        return intra + xi_f * cross_f, xi_b * cross_b

    def finish(i, o):
        o = o * lax.rsqrt(jnp.mean(o * o, axis=-1, keepdims=True) + RMS_EPS)
        o_ref[0, rows(i), :] = o.astype(o_ref.dtype)

    half = n_chunks // 2

    def approach(j, carry):
        a, b = j, n_chunks - 1 - j
        part_a, part_b = parts(a, b)
        acc_ref[rows(a), :] = part_a
        acc_ref[rows(b), :] = part_b
        return carry

    def cross(j, carry):
        a, b = half + j, half - 1 - j
        part_a, part_b = parts(a, b)
        finish(a, acc_ref[rows(a), :] + part_a)
        finish(b, acc_ref[rows(b), :] + part_b)
        return carry

    lax.fori_loop(0, half, approach, 0, unroll=True)
    lax.fori_loop(0, half, cross, 0, unroll=True)


def _ret_core(proj3d, log1m_decay):
    b, seq, _ = proj3d.shape
    dk, dv, heads = RET_QK_DIM, RET_V_DIM, RET_HEADS
    assert seq % (2 * RET_CHUNK) == 0
    k_blk0 = heads
    v_blk0 = 2 * heads * dk // dv
    return pl.pallas_call(
        functools.partial(_ret_core_kernel, chunk=RET_CHUNK),
        grid=(b, heads),
        in_specs=[
            pl.BlockSpec(memory_space=pltpu.SMEM),
            pl.BlockSpec((1, seq, dk), lambda bi, h: (bi, 0, h)),
            pl.BlockSpec((1, seq, dk), lambda bi, h: (bi, 0, k_blk0 + h)),
            pl.BlockSpec((1, seq, dv), lambda bi, h: (bi, 0, v_blk0 + h)),
        ],
        out_specs=pl.BlockSpec((1, seq, dv), lambda bi, h: (bi, 0, h)),
        out_shape=jax.ShapeDtypeStruct((b, seq, heads * dv), BF16),
        scratch_shapes=[pltpu.VMEM((seq, dv), F32), pltpu.VMEM((dk, dv), F32), pltpu.VMEM((dk, dv), F32)],
        compiler_params=_params("parallel", "parallel"),
        name="ret_core",
    )(log1m_decay, proj3d, proj3d, proj3d)


def _ret_out_kernel(o_ref, gate_ref, x_ref, w_ref, g_post_ref, *rest):
    *ffn_refs, out_ref = rest
    y = (_silu(gate_ref[...].astype(F32)) * o_ref[...].astype(F32)).astype(BF16)
    x = x_ref[...] + _rmsnorm(_dot(y, w_ref[...]), g_post_ref[...])
    out_ref[...] = _ffn_tail(x, ffn_refs)


def _ret_out(o2d, proj2d, x2d, w_out, g_post, ffn):
    t, d = x2d.shape
    vw = o2d.shape[1]
    gate_blk = proj2d.shape[1] // vw - 1
    tm = TOKEN_TILE
    return pl.pallas_call(
        _ret_out_kernel,
        grid=(t // tm,),
        in_specs=[
            pl.BlockSpec((tm, vw), lambda i: (i, 0)),
            pl.BlockSpec((tm, vw), lambda i: (i, gate_blk)),
            pl.BlockSpec((tm, d), lambda i: (i, 0)),
            _resident(w_out.shape),
            _resident((1, d)),
            *ffn.specs(),
        ],
        out_specs=pl.BlockSpec((tm, d), lambda i: (i, 0)),
        out_shape=jax.ShapeDtypeStruct((t, d), F32),
        compiler_params=_params("parallel"),
        name="ret_out_ffn",
    )(o2d, proj2d, x2d, w_out, g_post, *ffn)


def _retention_layer(x, g_pre, g_post, w_in, log1m_decay, w_out, ffn):
    b, seq, d = x.shape
    inv = 1.0 / (RET_ROPE_BASE ** jnp.linspace(0.0, 1.0, RET_QK_DIM // 2, dtype=F32))
    ang = jnp.arange(seq, dtype=F32)[:, None] * inv[None, :]
    x2d = x.reshape(b * seq, d)
    proj = _ret_proj(x2d, g_pre, w_in, jnp.cos(ang), jnp.sin(ang), seq)
    o = _ret_core(proj.reshape(b, seq, -1), log1m_decay)
    return _ret_out(o.reshape(b * seq, -1), proj, x2d, w_out, g_post, ffn).reshape(b, seq, d)


def _conv_in_kernel(x_ref, g_ref, w_ref, b_ref, u_ref, *, chunk):
    d = u_ref.shape[1]
    xn = _rmsnorm(x_ref[...], g_ref[...]).astype(BF16)
    for lo in range(0, d, chunk):
        a = _dot(xn, w_ref[:, lo:lo + chunk]) + b_ref[:, lo:lo + chunk]
        gate = _dot(xn, w_ref[:, d + lo:d + lo + chunk]) + b_ref[:, d + lo:d + lo + chunk]
        u_ref[:, lo:lo + chunk] = (a * jax.nn.sigmoid(gate)).astype(BF16)


def _conv_in(x2d, g, w_in, b_in):
    t, d = x2d.shape
    tm = TOKEN_TILE
    return pl.pallas_call(
        functools.partial(_conv_in_kernel, chunk=PROJ_COLS),
        grid=(t // tm,),
        in_specs=[
            pl.BlockSpec((tm, d), lambda i: (i, 0)),
            _resident((1, d)),
            _resident(w_in.shape),
            _resident(b_in.shape),
        ],
        out_specs=pl.BlockSpec((tm, d), lambda i: (i, 0)),
        out_shape=jax.ShapeDtypeStruct((t, d), BF16),
        compiler_params=_params("parallel"),
        name="conv_in",
    )(x2d, g, w_in, b_in)


def _conv_out_kernel(u_prev_ref, u_ref, u_next_ref, x_ref, w_dw_ref, b_dw_ref, ln_g_ref, ln_b_ref,
                     w_ref, b_out_ref, g_post_ref, g2_ref, f_in_ref, f_out_ref, g3_ref, shift_ref, out_ref, ext_ref, *,
                     strip):
    i = pl.program_id(1)
    tm = u_ref.shape[1]
    halo = u_prev_ref.shape[1]
    taps = w_dw_ref.shape[0]
    pad = (taps - 1) // 2
    zero = jnp.zeros((halo, u_ref.shape[2]), BF16)
    ext_ref[0:halo, :] = jnp.where(i > 0, u_prev_ref[0], zero)
    ext_ref[halo:halo + tm, :] = u_ref[0]
    ext_ref[halo + tm:, :] = jnp.where(i < pl.num_programs(1) - 1, u_next_ref[0], zero)
    rows_w = strip + 2 * halo
    strips = []
    for r0 in range(0, tm, strip):
        window = ext_ref[r0:r0 + rows_w, :]
        moved = _dot(shift_ref[...], window)
        aligned = window.astype(F32)
        acc = jnp.zeros((strip, u_ref.shape[2]), F32) + b_dw_ref[...]
        for res in range(SUBLANES):
            for k in range(taps):
                off = halo - pad + k
                if off % SUBLANES == res:
                    if res == 0:
                        src = aligned[off:off + strip, :]
                    else:
                        lo = (res - 1) * rows_w + off - res
                        src = moved[lo:lo + strip, :]
                    acc = acc + src * w_dw_ref[k:k + 1, :]
        strips.append(acc)
    acc = jnp.concatenate(strips, axis=0)
    mu = jnp.mean(acc, axis=-1, keepdims=True)
    cen = acc - mu
    var = jnp.mean(cen * cen, axis=-1, keepdims=True)
    y = cen * lax.rsqrt(var + LN_EPS) * ln_g_ref[...] + ln_b_ref[...]
    m = _dot(_silu(y).astype(BF16), w_ref[...]) + b_out_ref[...]
    x = x_ref[0] + _rmsnorm(m, g_post_ref[...])
    out_ref[0] = _ffn_tail(x, (g2_ref, f_in_ref, f_out_ref, g3_ref))


def _conv_out(u, x, w_dw, b_dw, ln_g, ln_b, w_out, b_out, g_post, ffn):
    b, seq, d = x.shape
    tm = TOKEN_TILE
    halo = CONV_HALO
    r = tm // halo
    last = seq // halo - 1
    assert halo >= (w_dw.shape[0] - 1) // 2 and tm % CONV_STRIP == 0
    rows_w = CONV_STRIP + 2 * halo
    out_row = lax.broadcasted_iota(jnp.int32, ((SUBLANES - 1) * rows_w, rows_w), 0)
    in_row = lax.broadcasted_iota(jnp.int32, ((SUBLANES - 1) * rows_w, rows_w), 1)
    shift = (in_row == out_row % rows_w + out_row // rows_w + 1).astype(BF16)
    return pl.pallas_call(
        functools.partial(_conv_out_kernel, strip=CONV_STRIP),
        grid=(b, seq // tm),
        in_specs=[
            pl.BlockSpec((1, halo, d), lambda bi, i: (bi, jnp.maximum(i * r - 1, 0), 0)),
            pl.BlockSpec((1, tm, d), lambda bi, i: (bi, i, 0)),
            pl.BlockSpec((1, halo, d), lambda bi, i: (bi, jnp.minimum((i + 1) * r, last), 0)),
            pl.BlockSpec((1, tm, d), lambda bi, i: (bi, i, 0)),
            _resident(w_dw.shape),
            _resident((1, d)),
            _resident((1, d)),
            _resident((1, d)),
            _resident(w_out.shape),
            _resident((1, d)),
            _resident((1, d)),
            *ffn.specs(),
            _resident(shift.shape),
        ],
        out_specs=pl.BlockSpec((1, tm, d), lambda bi, i: (bi, i, 0)),
        out_shape=jax.ShapeDtypeStruct((b, seq, d), F32),
        scratch_shapes=[pltpu.VMEM((tm + 2 * halo, d), BF16)],
        compiler_params=_params("parallel", "parallel"),
        name="conv_out_ffn",
    )(u, u, u, x, w_dw, b_dw, ln_g, ln_b, w_out, b_out, g_post, *ffn, shift)


def _conv_layer(x, g_pre, g_post, w_in, b_in, w_dw, b_dw, ln_g, ln_b, w_out, b_out, ffn):
    b, seq, d = x.shape
    u = _conv_in(x.reshape(b * seq, d), g_pre, w_in, b_in).reshape(b, seq, d)
    return _conv_out(u, x, w_dw, b_dw, ln_g, ln_b, w_out, b_out, g_post, ffn)


def _attn_proj_kernel(x_ref, g_ref, w_ref, c_ref, s_lo_ref, s_hi_ref, *rest, dils, rot_width, chunk):
    o_refs = rest[:len(dils)]
    xs_ref = rest[len(dils)]
    xp_refs = rest[len(dils) + 1:]
    tm = x_ref.shape[0]
    group_width = w_ref.shape[1] // len(dils)
    half = ROT_DIM // 2
    xn = _rmsnorm(x_ref[...], g_ref[...])
    for cb in range(xs_ref.shape[0]):
        xs_ref[cb] = xn[:, cb * LANES:(cb + 1) * LANES]
    staged = 0
    for g, dil in enumerate(dils):
        n = tm // dil
        if dil == 1:
            xp = xn.astype(BF16)
        else:
            xp_ref = xp_refs[staged]
            staged += 1
            for r in range(dil):
                for cb in range(xs_ref.shape[0]):
                    xp_ref[r * n:(r + 1) * n, cb * LANES:(cb + 1) * LANES] = (
                        xs_ref[cb, pl.ds(r, n, stride=dil), :].astype(BF16))
            xp = xp_ref[...]

        def permuted(tbl_ref):
            if dil == 1:
                return tbl_ref[...]
            return jnp.concatenate([tbl_ref[pl.ds(r, n, stride=dil), :] for r in range(dil)], axis=0)

        cmul = permuted(c_ref)
        s_lo = permuted(s_lo_ref)
        s_hi = permuted(s_hi_ref)
        for lo in range(0, group_width, chunk):
            a = _dot(xp, w_ref[:, g * group_width + lo:g * group_width + lo + chunk])
            for sub in range(0, chunk, LANES):
                piece = a[:, sub:sub + LANES]
                if lo < rot_width:
                    piece = (piece * cmul + pltpu.roll(piece, LANES - half, 1) * s_lo
                             + pltpu.roll(piece, half, 1) * s_hi)
                piece = piece.astype(BF16)
                for r in range(dil):
                    o_refs[g][0, r, :, lo + sub:lo + sub + LANES] = piece[r * n:(r + 1) * n, :]


def _attn_proj(x2d, g, w, tables, dils, seq):
    t, d = x2d.shape
    width = w.shape[1] // len(dils)
    tm = TOKEN_TILE
    blocks_per_seq = seq // tm
    b = t // seq
    table_spec = pl.BlockSpec((tm, LANES), lambda i: (i % blocks_per_seq, 0))
    return pl.pallas_call(
        functools.partial(_attn_proj_kernel, dils=dils, rot_width=2 * ATTN_HEADS * ATTN_HEAD_DIM, chunk=PROJ_COLS),
        grid=(t // tm,),
        in_specs=[
            pl.BlockSpec((tm, d), lambda i: (i, 0)),
            _resident((1, d)),
            _resident(w.shape),
            table_spec, table_spec, table_spec,
        ],
        out_specs=[pl.BlockSpec((1, dil, tm // dil, width),
                                lambda i: (i // blocks_per_seq, 0, i % blocks_per_seq, 0)) for dil in dils],
        out_shape=[jax.ShapeDtypeStruct((b, dil, seq // dil, width), BF16) for dil in dils],
        scratch_shapes=[pltpu.VMEM((d // LANES, tm, LANES), F32)]
        + [pltpu.VMEM((tm, d), BF16) for dil in dils if dil > 1],
        compiler_params=_params("parallel"),
        name="attn_proj",
    )(x2d, g, w, *tables)


def _band_attn_kernel(q_ref, kp_ref, kc_ref, kn_ref, vp_ref, vc_ref, vn_ref, o_ref, stat_ref, bias_ref, *,
                      heads, head_dim, radius, sub_len, tq):
    i = pl.program_id(1)
    rows_step = q_ref.shape[1]
    n_sub = rows_step // tq
    nk = tq + 2 * radius
    row = lax.broadcasted_iota(jnp.int32, (tq, nk), 0)
    col = lax.broadcasted_iota(jnp.int32, (tq, nk), 1)
    off = col - row
    band = (off >= 0) & (off <= 2 * radius)
    lane = lax.broadcasted_iota(jnp.int32, (tq, LANES), 1)
    first = lane < head_dim
    keep_first = first.astype(F32).astype(BF16)
    keep_second = (1.0 - first.astype(F32)).astype(BF16)
    ones = jnp.ones((nk, LANES), BF16)

    width = q_ref.shape[2]

    def window(prev_ref, cur_ref, next_ref, sq, r0):
        before = prev_ref[sq] if r0 == 0 else cur_ref[sq, r0 - radius:r0, :]
        after = next_ref[sq] if r0 + tq == rows_step else cur_ref[sq, r0 + tq:r0 + tq + radius, :]
        return jnp.concatenate([before, cur_ref[sq, r0:r0 + tq, :], after], axis=0)

    for sub in range(n_sub):
        r0 = sub * tq
        kpos = i * rows_step + r0 - radius + col
        bias_ref[sub] = jnp.where(band & (kpos >= 0) & (kpos < sub_len), 0.0, MASK_VALUE)
        for sq in range(q_ref.shape[0]):
            kwin = window(kp_ref, kc_ref, kn_ref, sq, r0)
            vwin = window(vp_ref, vc_ref, vn_ref, sq, r0)
            stat = jnp.zeros((tq, LANES), F32)
            for pair in range(heads // 2):
                cols = slice(pair * LANES, (pair + 1) * LANES)
                qp = q_ref[sq, r0:r0 + tq, cols]
                s2 = _dot_nt(jnp.concatenate([qp * keep_first, qp * keep_second], axis=0), kwin[:, cols])
                outs = []
                for k in range(2):
                    h = 2 * pair + k
                    s = s2[k * tq:(k + 1) * tq] + bias_ref[sub]
                    mx = jnp.max(s, axis=-1, keepdims=True)
                    p = jnp.exp2(s - mx).astype(BF16)
                    pv = _dot(p, jnp.concatenate([vwin[:, cols], ones], axis=1))
                    outs.append(pv[:, :LANES])
                    stat = jnp.where(lane == h, mx, stat)
                    stat = jnp.where(lane == heads + h, pv[:, LANES:], stat)
                o_lo = sq * width + pair * LANES
                o_ref[0, r0:r0 + tq, o_lo:o_lo + LANES] = jnp.where(first, outs[0], outs[1]).astype(o_ref.dtype)
            stat_ref[0, r0:r0 + tq, sq * LANES:(sq + 1) * LANES] = stat


def _band_attn(qkv, dil, radius):
    assert 2 * ATTN_HEADS <= LANES and 2 * ATTN_HEAD_DIM == LANES
    b, _, sub_len, w3 = qkv.shape
    width = w3 // 3
    tq = ATTN_Q_BLOCK
    rows_step = min(ATTN_Q_STEP, sub_len)
    nstep = sub_len // rows_step
    assert radius <= tq and rows_step % tq == 0 and sub_len % rows_step == 0 and rows_step % radius == 0
    flat = qkv.reshape(b * dil, sub_len, w3)
    per_step = rows_step // radius
    last_halo = sub_len // radius - 1
    n_seq = max(1, min(dil, ATTN_Q_STEP // rows_step))
    assert dil % n_seq == 0
    per_row = dil // n_seq

    def body(col):
        return pl.BlockSpec((n_seq, rows_step, width), lambda s, i: (s, i, col))

    def before(col):
        return pl.BlockSpec((n_seq, radius, width), lambda s, i: (s, jnp.maximum(i * per_step - 1, 0), col))

    def after(col):
        return pl.BlockSpec((n_seq, radius, width), lambda s, i: (s, jnp.minimum((i + 1) * per_step, last_halo), col))

    return pl.pallas_call(
        functools.partial(_band_attn_kernel, heads=ATTN_HEADS, head_dim=ATTN_HEAD_DIM, radius=radius,
                          sub_len=sub_len, tq=tq),
        grid=(b * per_row, nstep),
        in_specs=[body(0), before(1), body(1), after(1), before(2), body(2), after(2)],
        out_specs=[
            pl.BlockSpec((1, rows_step, n_seq * width), lambda s, i: (s // per_row, i, s % per_row)),
            pl.BlockSpec((1, rows_step, n_seq * LANES), lambda s, i: (s // per_row, i, s % per_row)),
        ],
        out_shape=[
            jax.ShapeDtypeStruct((b, sub_len, dil * width), BF16),
            jax.ShapeDtypeStruct((b, sub_len, dil * LANES), F32),
        ],
        scratch_shapes=[pltpu.VMEM((rows_step // tq, tq, tq + 2 * radius), F32)],
        compiler_params=_params("parallel", "parallel"),
        name=f"band_attn_d{dil}",
    )(flat, flat, flat, flat, flat, flat, flat)


def _attn_out_kernel(*refs, dils, head_dim):
    n_groups = len(dils)
    o_refs = refs[:n_groups]
    stat_refs = refs[n_groups:2 * n_groups]
    x_ref, w_ref, g_post_ref, *ffn_refs, out_ref = refs[2 * n_groups:2 * n_groups + 8]
    scratch = refs[2 * n_groups + 8:]
    tm, d = x_ref.shape[1], x_ref.shape[2]

    def natural(ref, dil, width, buf):
        if dil == 1:
            return ref[0].astype(F32)
        n = tm // dil
        for r in range(dil):
            for cb in range(width // LANES):
                lo = r * width + cb * LANES
                buf[cb, pl.ds(r, n, stride=dil), :] = ref[0, :, lo:lo + LANES].astype(F32)
        return jnp.concatenate([buf[cb] for cb in range(width // LANES)], axis=1)

    outs, stats = [], []
    k = 0
    for g, dil in enumerate(dils):
        if dil == 1:
            outs.append(natural(o_refs[g], 1, d, None))
            stats.append(natural(stat_refs[g], 1, LANES, None))
        else:
            outs.append(natural(o_refs[g], dil, d, scratch[k]))
            stats.append(natural(stat_refs[g], dil, LANES, scratch[k + 1]))
            k += 2
    heads = d // head_dim
    head_lane = lax.broadcasted_iota(jnp.int32, (tm, LANES), 1) < heads
    mx = functools.reduce(jnp.maximum, stats)
    es = [jnp.exp2(l - mx) for l in stats]
    dens = [pltpu.roll(l, LANES - heads, 1) for l in stats]
    tot = functools.reduce(lambda a, b_: a + b_, [e * dn for e, dn in zip(es, dens)])
    tot = jnp.where(head_lane, tot, 1.0)
    expand = (lax.broadcasted_iota(jnp.int32, (2 * LANES, d), 1) // head_dim
              == lax.broadcasted_iota(jnp.int32, (2 * LANES, d), 0) % LANES).astype(BF16)
    mixed = jnp.zeros((tm, d), F32)
    for e, o in zip(es, outs):
        wgt = jnp.where(head_lane, e / tot, 0.0)
        hi = wgt.astype(BF16)
        lo = (wgt - hi.astype(F32)).astype(BF16)
        mixed = mixed + _dot(jnp.concatenate([hi, lo], axis=1), expand) * o
    m = _dot(mixed.astype(BF16), w_ref[...])
    x = x_ref[0] + _rmsnorm(m, g_post_ref[...])
    out_ref[0] = _ffn_tail(x, ffn_refs)


def _attn_out(os_, stats, x, w_out, g_post, dils, ffn):
    b, seq, d = x.shape
    tm = TOKEN_TILE
    in_specs = []
    for dil in dils:
        in_specs.append(pl.BlockSpec((1, tm // dil, dil * d), lambda bi, i: (bi, i, 0)))
    for dil in dils:
        in_specs.append(pl.BlockSpec((1, tm // dil, dil * LANES), lambda bi, i: (bi, i, 0)))
    in_specs += [pl.BlockSpec((1, tm, d), lambda bi, i: (bi, i, 0)), _resident(w_out.shape), _resident((1, d)),
                 *ffn.specs()]
    scratch = []
    for dil in dils:
        if dil > 1:
            scratch += [pltpu.VMEM((d // LANES, tm, LANES), F32), pltpu.VMEM((1, tm, LANES), F32)]
    return pl.pallas_call(
        functools.partial(_attn_out_kernel, dils=dils, head_dim=ATTN_HEAD_DIM),
        grid=(b, seq // tm),
        in_specs=in_specs,
        out_specs=pl.BlockSpec((1, tm, d), lambda bi, i: (bi, i, 0)),
        out_shape=jax.ShapeDtypeStruct((b, seq, d), F32),
        scratch_shapes=scratch,
        compiler_params=_params("parallel", "parallel"),
        name="attn_out_ffn",
    )(*os_, *stats, x, w_out, g_post, *ffn)


def _rotary_tables(seq):
    inv = ROPE_THETA ** (-jnp.arange(0, ROT_DIM, 2, dtype=F32) / ROT_DIM)
    ang = jnp.arange(seq, dtype=F32)[:, None] * inv[None, :]
    cos, sin = jnp.cos(ang), jnp.sin(ang)
    half = ROT_DIM // 2
    rest = ATTN_HEAD_DIM - ROT_DIM
    ones = jnp.ones((seq, rest), F32)
    zeros_h = jnp.zeros((seq, half), F32)
    zeros_r = jnp.zeros((seq, rest), F32)
    per_head = (
        jnp.concatenate([cos, cos, ones], axis=1),
        jnp.concatenate([-sin, zeros_h, zeros_r], axis=1),
        jnp.concatenate([zeros_h, sin, zeros_r], axis=1),
    )
    return [jnp.tile(tbl, (1, LANES // ATTN_HEAD_DIM)) for tbl in per_head]


def _attention_layer(x, g_pre, g_post, w_in, w_out, ffn):
    b, seq, d = x.shape
    x2d = x.reshape(b * seq, d)
    dils = tuple(dil for _, dil in DILATION_GROUPS)
    os_, stats = [], []
    qkvs = _attn_proj(x2d, g_pre, w_in, _rotary_tables(seq), dils, seq)
    for qkv, (window, dil) in zip(qkvs, DILATION_GROUPS):
        o, stat = _band_attn(qkv, dil, window // (2 * dil))
        os_.append(o)
        stats.append(stat)
    return _attn_out(os_, stats, x, w_out, g_post, dils, ffn)


def _scaled_bf16(w, lo, hi, scale):
    col = jnp.arange(w.shape[-1])
    return (w * jnp.where((col >= lo) & (col < hi), scale, 1.0).astype(w.dtype)).astype(BF16)


def kernel(x, norm_w, ffn_w_in, ffn_w_out, ret_w_in, ret_log1m_decay, ret_w_out, conv_w_in, conv_b_in, conv_w_dw,
           conv_b_dw, conv_ln_g, conv_ln_b, conv_w_out, conv_b_out, attn_w_in, attn_w_out):
    depth = norm_w.shape[0]
    n_mixers = 3
    qk_w = RET_HEADS * RET_QK_DIM
    attn_gw = 3 * ATTN_HEADS * ATTN_HEAD_DIM
    attn_col = jnp.arange(attn_w_in.shape[-1]) % attn_gw
    attn_scale = jnp.where(attn_col < ATTN_HEADS * ATTN_HEAD_DIM, ATTN_HEAD_DIM ** -0.5 * LOG2_E, 1.0).astype(F32)
    row = lambda v: v.reshape(1, -1)
    for i in range(depth):
        kind, j = i % n_mixers, i // n_mixers
        g_pre, g_post = row(norm_w[i, 0]), row(norm_w[i, 1])
        ffn = Ffn(row(norm_w[i, 2]), ffn_w_in[i].astype(BF16), ffn_w_out[i].astype(BF16), row(norm_w[i, 3]))
        if kind == 0:
            w_in = _scaled_bf16(ret_w_in[j], qk_w, 2 * qk_w, RET_QK_DIM ** -0.5)
            x = _retention_layer(x, g_pre, g_post, w_in, ret_log1m_decay[j], ret_w_out[j].astype(BF16), ffn)
        elif kind == 1:
            x = _conv_layer(x, g_pre, g_post, conv_w_in[j].astype(BF16), row(conv_b_in[j]), conv_w_dw[j],
                            row(conv_b_dw[j]), row(conv_ln_g[j]), row(conv_ln_b[j]), conv_w_out[j].astype(BF16),
                            row(conv_b_out[j]), ffn)
        else:
            w_in = (attn_w_in[j] * attn_scale).astype(BF16)
            x = _attention_layer(x, g_pre, g_post, w_in, attn_w_out[j].astype(BF16), ffn)
    return x
```

```python
import functools
from typing import NamedTuple

import jax
import jax.numpy as jnp
from jax import lax
from jax.experimental import pallas as pl
from jax.experimental.pallas import tpu as pltpu

F32 = jnp.float32
BF16 = jnp.bfloat16

RMS_EPS = 1e-6
LN_EPS = 1e-5
MASK_VALUE = -1e30
LOG2_E = 1.4426950408889634

RET_HEADS = 4
RET_QK_DIM = 256
RET_V_DIM = 512
RET_ROPE_BASE = 10000.0
ATTN_HEADS = 16
ATTN_HEAD_DIM = 64
DILATION_GROUPS = ((128, 1), (512, 4), (2048, 16))
ROPE_THETA = 500000.0
ROT_DIM = ATTN_HEAD_DIM // 4

LANES = 128
SUBLANES = 8
V7X_VMEM_LIMIT_BYTES = 56 * 1024 * 1024

TOKEN_TILE = 512
RET_CHUNK = 256
ATTN_Q_BLOCK = 128
ATTN_Q_STEP = 1024
CONV_HALO = 16
CONV_STRIP = 128
PROJ_COLS = 512


def _params(*semantics):
    return pltpu.CompilerParams(dimension_semantics=semantics, vmem_limit_bytes=V7X_VMEM_LIMIT_BYTES)


def _resident(shape):
    zeros = (0,) * len(shape)
    return pl.BlockSpec(shape, lambda *_: zeros, pipeline_mode=pl.Buffered(1))


def _rmsnorm(x, g):
    return x * lax.rsqrt(jnp.mean(x * x, axis=-1, keepdims=True) + RMS_EPS) * g


def _silu(x):
    return x * jax.nn.sigmoid(x)


def _dot(a, b):
    return jnp.dot(a, b, preferred_element_type=F32)


def _dot_nt(a, b):
    return lax.dot_general(a, b, (((1,), (1,)), ((), ())), preferred_element_type=F32)


def _dot_tn(a, b):
    return lax.dot_general(a, b, (((0,), (0,)), ((), ())), preferred_element_type=F32)


def _ffn_hidden_chunk(hidden):
    for cand in (512, 256, 128):
        if hidden % cand == 0:
            return cand
    raise ValueError(f"hidden width {hidden} is not a multiple of {LANES}")


class Ffn(NamedTuple):
    g_pre: jax.Array
    w_in: jax.Array
    w_out: jax.Array
    g_post: jax.Array

    def specs(self):
        return [_resident(a.shape) for a in self]


def _ffn_tail(x, ffn_refs):
    g_pre_ref, w_in_ref, w_out_ref, g_post_ref = ffn_refs
    hidden = w_out_ref.shape[0]
    chunk = _ffn_hidden_chunk(hidden)
    xn = _rmsnorm(x, g_pre_ref[...]).astype(BF16)
    acc = jnp.zeros(x.shape, F32)
    for lo in range(0, hidden, chunk):
        a = _dot(xn, w_in_ref[:, lo:lo + chunk])
        b = _dot(xn, w_in_ref[:, hidden + lo:hidden + lo + chunk])
        h = (_silu(a) * b).astype(BF16)
        acc = acc + _dot(h, w_out_ref[lo:lo + chunk, :])
    return x + _rmsnorm(acc, g_post_ref[...])


def _ret_proj_kernel(x_ref, g_ref, w_ref, cos_ref, sin_ref, o_ref, *, rot_width, head_dim, plain_chunk):
    xn = _rmsnorm(x_ref[...], g_ref[...]).astype(BF16)
    cos = cos_ref[...]
    sin = sin_ref[...]
    half = head_dim // 2
    for h in range(rot_width // head_dim):
        lo = h * head_dim
        a = _dot(xn, w_ref[:, lo:lo + head_dim])
        x1 = a[:, :half]
        x2 = a[:, half:]
        o_ref[:, lo:lo + half] = (x1 * cos - x2 * sin).astype(BF16)
        o_ref[:, lo + half:lo + head_dim] = (x2 * cos + x1 * sin).astype(BF16)
    width = w_ref.shape[1]
    for lo in range(rot_width, width, plain_chunk):
        o_ref[:, lo:lo + plain_chunk] = _dot(xn, w_ref[:, lo:lo + plain_chunk]).astype(BF16)


def _ret_proj(x2d, g, w, cos, sin, seq):
    t, d = x2d.shape
    width = w.shape[1]
    tm = TOKEN_TILE
    blocks_per_seq = seq // tm
    half = RET_QK_DIM // 2
    return pl.pallas_call(
        functools.partial(_ret_proj_kernel, rot_width=2 * RET_HEADS * RET_QK_DIM, head_dim=RET_QK_DIM,
                          plain_chunk=PROJ_COLS),
        grid=(t // tm,),
        in_specs=[
            pl.BlockSpec((tm, d), lambda i: (i, 0)),
            _resident((1, d)),
            _resident(w.shape),
            pl.BlockSpec((tm, half), lambda i: (i % blocks_per_seq, 0)),
            pl.BlockSpec((tm, half), lambda i: (i % blocks_per_seq, 0)),
        ],
        out_specs=pl.BlockSpec((tm, width), lambda i: (i, 0)),
        out_shape=jax.ShapeDtypeStruct((t, width), BF16),
        compiler_params=_params("parallel"),
        name="ret_proj",
    )(x2d, g, w, cos, sin)


def _ret_core_kernel(decay_ref, q_ref, k_ref, v_ref, o_ref, acc_ref, stf_ref, stb_ref, *, chunk):
    head = pl.program_id(1)
    seq = q_ref.shape[1]
    n_chunks = seq // chunk
    c = chunk

    def log_gamma(direction):
        log1m = jnp.full((c, 1), decay_ref[direction, head], F32)
        return jnp.log(1.0 - jnp.exp(log1m))

    lgf = log_gamma(0)
    lgb = log_gamma(1)
    idx = lax.broadcasted_iota(jnp.int32, (c, 1), 0).astype(F32)
    xi_f = jnp.exp(lgf * (idx + 1.0))
    zeta_f = jnp.exp(lgf * (c - 1.0 - idx))
    decay_f = jnp.exp(lgf * float(c))
    xi_b = jnp.exp(lgb * (c - idx))
    zeta_b = jnp.exp(lgb * idx)
    decay_b = jnp.exp(lgb * float(c))
    diff = (lax.broadcasted_iota(jnp.int32, (c, c), 0) - lax.broadcasted_iota(jnp.int32, (c, c), 1)).astype(F32)
    dmat = jnp.exp(jnp.where(diff >= 0.0, lgf, -lgb) * diff)

    def rows(i):
        return pl.ds(pl.multiple_of(i * c, c), c)

    def state_update(st, decay, kc, zeta, vc):
        kz = (kc.astype(F32) * zeta).astype(BF16)
        return st * decay + _dot_tn(kz, vc)

    stf_ref[...] = jnp.zeros(stf_ref.shape, F32)
    stb_ref[...] = jnp.zeros(stb_ref.shape, F32)

    def parts(a, b):
        ra, rb = rows(a), rows(b)
        qa, ka, va = q_ref[0, ra, :], k_ref[0, ra, :], v_ref[0, ra, :]
        qb, kb, vb = q_ref[0, rb, :], k_ref[0, rb, :], v_ref[0, rb, :]
        stf = stf_ref[...]
        stb = stb_ref[...]
        scores = _dot_nt(qa, ka)
        cross_f = _dot(qa, stf.astype(BF16))
        cross_b = _dot(qb, stb.astype(BF16))
        stf_ref[...] = state_update(stf, decay_f, ka, zeta_f, va)
        stb_ref[...] = state_update(stb, decay_b, kb, zeta_b, vb)
        intra = _dot((scores * dmat).astype(BF16), va)
        return intra + xi_f * cross_f, xi_b * cross_b

    def finish(i, o):
        o = o * lax.rsqrt(jnp.mean(o * o, axis=-1, keepdims=True) + RMS_EPS)
        o_ref[0, rows(i), :] = o.astype(o_ref.dtype)

    half = n_chunks // 2

    def approach(j, carry):
        a, b = j, n_chunks - 1 - j
        part_a, part_b = parts(a, b)
        acc_ref[rows(a), :] = part_a
        acc_ref[rows(b), :] = part_b
        return carry

    def cross(j, carry):
        a, b = half + j, half - 1 - j
        part_a, part_b = parts(a, b)
        finish(a, acc_ref[rows(a), :] + part_a)
        finish(b, acc_ref[rows(b), :] + part_b)
        return carry

    lax.fori_loop(0, half, approach, 0, unroll=True)
    lax.fori_loop(0, half, cross, 0, unroll=True)


def _ret_core(proj3d, log1m_decay):
    b, seq, _ = proj3d.shape
    dk, dv, heads = RET_QK_DIM, RET_V_DIM, RET_HEADS
    assert seq % (2 * RET_CHUNK) == 0
    k_blk0 = heads
    v_blk0 = 2 * heads * dk // dv
    return pl.pallas_call(
        functools.partial(_ret_core_kernel, chunk=RET_CHUNK),
        grid=(b, heads),
        in_specs=[
            pl.BlockSpec(memory_space=pltpu.SMEM),
            pl.BlockSpec((1, seq, dk), lambda bi, h: (bi, 0, h)),
            pl.BlockSpec((1, seq, dk), lambda bi, h: (bi, 0, k_blk0 + h)),
            pl.BlockSpec((1, seq, dv), lambda bi, h: (bi, 0, v_blk0 + h)),
        ],
        out_specs=pl.BlockSpec((1, seq, dv), lambda bi, h: (bi, 0, h)),
        out_shape=jax.ShapeDtypeStruct((b, seq, heads * dv), BF16),
        scratch_shapes=[pltpu.VMEM((seq, dv), F32), pltpu.VMEM((dk, dv), F32), pltpu.VMEM((dk, dv), F32)],
        compiler_params=_params("parallel", "parallel"),
        name="ret_core",
    )(log1m_decay, proj3d, proj3d, proj3d)


def _ret_out_kernel(o_ref, gate_ref, x_ref, w_ref, g_post_ref, *rest):
    *ffn_refs, out_ref = rest
    y = (_silu(gate_ref[...].astype(F32)) * o_ref[...].astype(F32)).astype(BF16)
    x = x_ref[...] + _rmsnorm(_dot(y, w_ref[...]), g_post_ref[...])
    out_ref[...] = _ffn_tail(x, ffn_refs)


def _ret_out(o2d, proj2d, x2d, w_out, g_post, ffn):
    t, d = x2d.shape
    vw = o2d.shape[1]
    gate_blk = proj2d.shape[1] // vw - 1
    tm = TOKEN_TILE
    return pl.pallas_call(
        _ret_out_kernel,
        grid=(t // tm,),
        in_specs=[
            pl.BlockSpec((tm, vw), lambda i: (i, 0)),
            pl.BlockSpec((tm, vw), lambda i: (i, gate_blk)),
            pl.BlockSpec((tm, d), lambda i: (i, 0)),
            _resident(w_out.shape),
            _resident((1, d)),
            *ffn.specs(),
        ],
        out_specs=pl.BlockSpec((tm, d), lambda i: (i, 0)),
        out_shape=jax.ShapeDtypeStruct((t, d), F32),
        compiler_params=_params("parallel"),
        name="ret_out_ffn",
    )(o2d, proj2d, x2d, w_out, g_post, *ffn)


def _retention_layer(x, g_pre, g_post, w_in, log1m_decay, w_out, ffn):
    b, seq, d = x.shape
    inv = 1.0 / (RET_ROPE_BASE ** jnp.linspace(0.0, 1.0, RET_QK_DIM // 2, dtype=F32))
    ang = jnp.arange(seq, dtype=F32)[:, None] * inv[None, :]
    x2d = x.reshape(b * seq, d)
    proj = _ret_proj(x2d, g_pre, w_in, jnp.cos(ang), jnp.sin(ang), seq)
    o = _ret_core(proj.reshape(b, seq, -1), log1m_decay)
    return _ret_out(o.reshape(b * seq, -1), proj, x2d, w_out, g_post, ffn).reshape(b, seq, d)


def _conv_in_kernel(x_ref, g_ref, w_ref, b_ref, u_ref, *, chunk):
    d = u_ref.shape[1]
    xn = _rmsnorm(x_ref[...], g_ref[...]).astype(BF16)
    for lo in range(0, d, chunk):
        a = _dot(xn, w_ref[:, lo:lo + chunk]) + b_ref[:, lo:lo + chunk]
        gate = _dot(xn, w_ref[:, d + lo:d + lo + chunk]) + b_ref[:, d + lo:d + lo + chunk]
        u_ref[:, lo:lo + chunk] = (a * jax.nn.sigmoid(gate)).astype(BF16)


def _conv_in(x2d, g, w_in, b_in):
    t, d = x2d.shape
    tm = TOKEN_TILE
    return pl.pallas_call(
        functools.partial(_conv_in_kernel, chunk=PROJ_COLS),
        grid=(t // tm,),
        in_specs=[
            pl.BlockSpec((tm, d), lambda i: (i, 0)),
            _resident((1, d)),
            _resident(w_in.shape),
            _resident(b_in.shape),
        ],
        out_specs=pl.BlockSpec((tm, d), lambda i: (i, 0)),
        out_shape=jax.ShapeDtypeStruct((t, d), BF16),
        compiler_params=_params("parallel"),
        name="conv_in",
    )(x2d, g, w_in, b_in)


def _conv_out_kernel(u_prev_ref, u_ref, u_next_ref, x_ref, w_dw_ref, b_dw_ref, ln_g_ref, ln_b_ref,
                     w_ref, b_out_ref, g_post_ref, g2_ref, f_in_ref, f_out_ref, g3_ref, shift_ref, out_ref, ext_ref, *,
                     strip):
    i = pl.program_id(1)
    tm = u_ref.shape[1]
    halo = u_prev_ref.shape[1]
    taps = w_dw_ref.shape[0]
    pad = (taps - 1) // 2
    zero = jnp.zeros((halo, u_ref.shape[2]), BF16)
    ext_ref[0:halo, :] = jnp.where(i > 0, u_prev_ref[0], zero)
    ext_ref[halo:halo + tm, :] = u_ref[0]
    ext_ref[halo + tm:, :] = jnp.where(i < pl.num_programs(1) - 1, u_next_ref[0], zero)
    rows_w = strip + 2 * halo
    strips = []
    for r0 in range(0, tm, strip):
        window = ext_ref[r0:r0 + rows_w, :]
        moved = _dot(shift_ref[...], window)
        aligned = window.astype(F32)
        acc = jnp.zeros((strip, u_ref.shape[2]), F32) + b_dw_ref[...]
        for res in range(SUBLANES):
            for k in range(taps):
                off = halo - pad + k
                if off % SUBLANES == res:
                    if res == 0:
                        src = aligned[off:off + strip, :]
                    else:
                        lo = (res - 1) * rows_w + off - res
                        src = moved[lo:lo + strip, :]
                    acc = acc + src * w_dw_ref[k:k + 1, :]
        strips.append(acc)
    acc = jnp.concatenate(strips, axis=0)
    mu = jnp.mean(acc, axis=-1, keepdims=True)
    cen = acc - mu
    var = jnp.mean(cen * cen, axis=-1, keepdims=True)
    y = cen * lax.rsqrt(var + LN_EPS) * ln_g_ref[...] + ln_b_ref[...]
    m = _dot(_silu(y).astype(BF16), w_ref[...]) + b_out_ref[...]
    x = x_ref[0] + _rmsnorm(m, g_post_ref[...])
    out_ref[0] = _ffn_tail(x, (g2_ref, f_in_ref, f_out_ref, g3_ref))


def _conv_out(u, x, w_dw, b_dw, ln_g, ln_b, w_out, b_out, g_post, ffn):
    b, seq, d = x.shape
    tm = TOKEN_TILE
    halo = CONV_HALO
    r = tm // halo
    last = seq // halo - 1
    assert halo >= (w_dw.shape[0] - 1) // 2 and tm % CONV_STRIP == 0
    rows_w = CONV_STRIP + 2 * halo
    out_row = lax.broadcasted_iota(jnp.int32, ((SUBLANES - 1) * rows_w, rows_w), 0)
    in_row = lax.broadcasted_iota(jnp.int32, ((SUBLANES - 1) * rows_w, rows_w), 1)
    shift = (in_row == out_row % rows_w + out_row // rows_w + 1).astype(BF16)
    return pl.pallas_call(
        functools.partial(_conv_out_kernel, strip=CONV_STRIP),
        grid=(b, seq // tm),
        in_specs=[
            pl.BlockSpec((1, halo, d), lambda bi, i: (bi, jnp.maximum(i * r - 1, 0), 0)),
            pl.BlockSpec((1, tm, d), lambda bi, i: (bi, i, 0)),
            pl.BlockSpec((1, halo, d), lambda bi, i: (bi, jnp.minimum((i + 1) * r, last), 0)),
            pl.BlockSpec((1, tm, d), lambda bi, i: (bi, i, 0)),
            _resident(w_dw.shape),
            _resident((1, d)),
            _resident((1, d)),
            _resident((1, d)),
            _resident(w_out.shape),
            _resident((1, d)),
            _resident((1, d)),
            *ffn.specs(),
            _resident(shift.shape),
        ],
        out_specs=pl.BlockSpec((1, tm, d), lambda bi, i: (bi, i, 0)),
        out_shape=jax.ShapeDtypeStruct((b, seq, d), F32),
        scratch_shapes=[pltpu.VMEM((tm + 2 * halo, d), BF16)],
        compiler_params=_params("parallel", "parallel"),
        name="conv_out_ffn",
    )(u, u, u, x, w_dw, b_dw, ln_g, ln_b, w_out, b_out, g_post, *ffn, shift)


def _conv_layer(x, g_pre, g_post, w_in, b_in, w_dw, b_dw, ln_g, ln_b, w_out, b_out, ffn):
    b, seq, d = x.shape
    u = _conv_in(x.reshape(b * seq, d), g_pre, w_in, b_in).reshape(b, seq, d)
    return _conv_out(u, x, w_dw, b_dw, ln_g, ln_b, w_out, b_out, g_post, ffn)


def _attn_proj_kernel(x_ref, g_ref, w_ref, c_ref, s_lo_ref, s_hi_ref, *rest, dils, rot_width, chunk):
    o_refs = rest[:len(dils)]
    xs_ref = rest[len(dils)]
    xp_refs = rest[len(dils) + 1:]
    tm = x_ref.shape[0]
    group_width = w_ref.shape[1] // len(dils)
    half = ROT_DIM // 2
    xn = _rmsnorm(x_ref[...], g_ref[...])
    for cb in range(xs_ref.shape[0]):
        xs_ref[cb] = xn[:, cb * LANES:(cb + 1) * LANES]
    staged = 0
    for g, dil in enumerate(dils):
        n = tm // dil
        if dil == 1:
            xp = xn.astype(BF16)
        else:
            xp_ref = xp_refs[staged]
            staged += 1
            for r in range(dil):
                for cb in range(xs_ref.shape[0]):
                    xp_ref[r * n:(r + 1) * n, cb * LANES:(cb + 1) * LANES] = (
                        xs_ref[cb, pl.ds(r, n, stride=dil), :].astype(BF16))
            xp = xp_ref[...]

        def permuted(tbl_ref):
            if dil == 1:
                return tbl_ref[...]
            return jnp.concatenate([tbl_ref[pl.ds(r, n, stride=dil), :] for r in range(dil)], axis=0)

        cmul = permuted(c_ref)
        s_lo = permuted(s_lo_ref)
        s_hi = permuted(s_hi_ref)
        for lo in range(0, group_width, chunk):
            a = _dot(xp, w_ref[:, g * group_width + lo:g * group_width + lo + chunk])
            for sub in range(0, chunk, LANES):
                piece = a[:, sub:sub + LANES]
                if lo < rot_width:
                    piece = (piece * cmul + pltpu.roll(piece, LANES - half, 1) * s_lo
                             + pltpu.roll(piece, half, 1) * s_hi)
                piece = piece.astype(BF16)
                for r in range(dil):
                    o_refs[g][0, r, :, lo + sub:lo + sub + LANES] = piece[r * n:(r + 1) * n, :]


def _attn_proj(x2d, g, w, tables, dils, seq):
    t, d = x2d.shape
    width = w.shape[1] // len(dils)
    tm = TOKEN_TILE
    blocks_per_seq = seq // tm
    b = t // seq
    table_spec = pl.BlockSpec((tm, LANES), lambda i: (i % blocks_per_seq, 0))
    return pl.pallas_call(
        functools.partial(_attn_proj_kernel, dils=dils, rot_width=2 * ATTN_HEADS * ATTN_HEAD_DIM, chunk=PROJ_COLS),
        grid=(t // tm,),
        in_specs=[
            pl.BlockSpec((tm, d), lambda i: (i, 0)),
            _resident((1, d)),
            _resident(w.shape),
            table_spec, table_spec, table_spec,
        ],
        out_specs=[pl.BlockSpec((1, dil, tm // dil, width),
                                lambda i: (i // blocks_per_seq, 0, i % blocks_per_seq, 0)) for dil in dils],
        out_shape=[jax.ShapeDtypeStruct((b, dil, seq // dil, width), BF16) for dil in dils],
        scratch_shapes=[pltpu.VMEM((d // LANES, tm, LANES), F32)]
        + [pltpu.VMEM((tm, d), BF16) for dil in dils if dil > 1],
        compiler_params=_params("parallel"),
        name="attn_proj",
    )(x2d, g, w, *tables)


def _band_attn_kernel(q_ref, kp_ref, kc_ref, kn_ref, vp_ref, vc_ref, vn_ref, o_ref, stat_ref, bias_ref, *,
                      heads, head_dim, radius, sub_len, tq):
    i = pl.program_id(1)
    rows_step = q_ref.shape[1]
    n_sub = rows_step // tq
    nk = tq + 2 * radius
    row = lax.broadcasted_iota(jnp.int32, (tq, nk), 0)
    col = lax.broadcasted_iota(jnp.int32, (tq, nk), 1)
    off = col - row
    band = (off >= 0) & (off <= 2 * radius)
    lane = lax.broadcasted_iota(jnp.int32, (tq, LANES), 1)
    first = lane < head_dim
    keep_first = first.astype(F32).astype(BF16)
    keep_second = (1.0 - first.astype(F32)).astype(BF16)
    ones = jnp.ones((nk, LANES), BF16)

    width = q_ref.shape[2]

    def window(prev_ref, cur_ref, next_ref, sq, r0):
        before = prev_ref[sq] if r0 == 0 else cur_ref[sq, r0 - radius:r0, :]
        after = next_ref[sq] if r0 + tq == rows_step else cur_ref[sq, r0 + tq:r0 + tq + radius, :]
        return jnp.concatenate([before, cur_ref[sq, r0:r0 + tq, :], after], axis=0)

    for sub in range(n_sub):
        r0 = sub * tq
        kpos = i * rows_step + r0 - radius + col
        bias_ref[sub] = jnp.where(band & (kpos >= 0) & (kpos < sub_len), 0.0, MASK_VALUE)
        for sq in range(q_ref.shape[0]):
            kwin = window(kp_ref, kc_ref, kn_ref, sq, r0)
            vwin = window(vp_ref, vc_ref, vn_ref, sq, r0)
            stat = jnp.zeros((tq, LANES), F32)
            for pair in range(heads // 2):
                cols = slice(pair * LANES, (pair + 1) * LANES)
                qp = q_ref[sq, r0:r0 + tq, cols]
                s2 = _dot_nt(jnp.concatenate([qp * keep_first, qp * keep_second], axis=0), kwin[:, cols])
                outs = []
                for k in range(2):
                    h = 2 * pair + k
                    s = s2[k * tq:(k + 1) * tq] + bias_ref[sub]
                    mx = jnp.max(s, axis=-1, keepdims=True)
                    p = jnp.exp2(s - mx).astype(BF16)
                    pv = _dot(p, jnp.concatenate([vwin[:, cols], ones], axis=1))
                    outs.append(pv[:, :LANES])
                    stat = jnp.where(lane == h, mx, stat)
                    stat = jnp.where(lane == heads + h, pv[:, LANES:], stat)
                o_lo = sq * width + pair * LANES
                o_ref[0, r0:r0 + tq, o_lo:o_lo + LANES] = jnp.where(first, outs[0], outs[1]).astype(o_ref.dtype)
            stat_ref[0, r0:r0 + tq, sq * LANES:(sq + 1) * LANES] = stat


def _band_attn(qkv, dil, radius):
    assert 2 * ATTN_HEADS <= LANES and 2 * ATTN_HEAD_DIM == LANES
    b, _, sub_len, w3 = qkv.shape
    width = w3 // 3
    tq = ATTN_Q_BLOCK
    rows_step = min(ATTN_Q_STEP, sub_len)
    nstep = sub_len // rows_step
    assert radius <= tq and rows_step % tq == 0 and sub_len % rows_step == 0 and rows_step % radius == 0
    flat = qkv.reshape(b * dil, sub_len, w3)
    per_step = rows_step // radius
    last_halo = sub_len // radius - 1
    n_seq = max(1, min(dil, ATTN_Q_STEP // rows_step))
    assert dil % n_seq == 0
    per_row = dil // n_seq

    def body(col):
        return pl.BlockSpec((n_seq, rows_step, width), lambda s, i: (s, i, col))

    def before(col):
        return pl.BlockSpec((n_seq, radius, width), lambda s, i: (s, jnp.maximum(i * per_step - 1, 0), col))

    def after(col):
        return pl.BlockSpec((n_seq, radius, width), lambda s, i: (s, jnp.minimum((i + 1) * per_step, last_halo), col))

    return pl.pallas_call(
        functools.partial(_band_attn_kernel, heads=ATTN_HEADS, head_dim=ATTN_HEAD_DIM, radius=radius,
                          sub_len=sub_len, tq=tq),
        grid=(b * per_row, nstep),
        in_specs=[body(0), before(1), body(1), after(1), before(2), body(2), after(2)],
        out_specs=[
            pl.BlockSpec((1, rows_step, n_seq * width), lambda s, i: (s // per_row, i, s % per_row)),
            pl.BlockSpec((1, rows_step, n_seq * LANES), lambda s, i: (s // per_row, i, s % per_row)),
        ],
        out_shape=[
            jax.ShapeDtypeStruct((b, sub_len, dil * width), BF16),
            jax.ShapeDtypeStruct((b, sub_len, dil * LANES), F32),
        ],
        scratch_shapes=[pltpu.VMEM((rows_step // tq, tq, tq + 2 * radius), F32)],
        compiler_params=_params("parallel", "parallel"),
        name=f"band_attn_d{dil}",
    )(flat, flat, flat, flat, flat, flat, flat)


def _attn_out_kernel(*refs, dils, head_dim):
    n_groups = len(dils)
    o_refs = refs[:n_groups]
    stat_refs = refs[n_groups:2 * n_groups]
    x_ref, w_ref, g_post_ref, *ffn_refs, expand_ref, out_ref = refs[2 * n_groups:2 * n_groups + 9]
    scratch = refs[2 * n_groups + 9:]
    tm, d = x_ref.shape[1], x_ref.shape[2]

    def natural(ref, dil, width, buf):
        if dil == 1:
            return ref[0].astype(F32)
        n = tm // dil
        for r in range(dil):
            for cb in range(width // LANES):
                lo = r * width + cb * LANES
                buf[cb, pl.ds(r, n, stride=dil), :] = ref[0, :, lo:lo + LANES].astype(F32)
        return jnp.concatenate([buf[cb] for cb in range(width // LANES)], axis=1)

    outs, stats = [], []
    k = 0
    for g, dil in enumerate(dils):
        if dil == 1:
            outs.append(natural(o_refs[g], 1, d, None))
            stats.append(natural(stat_refs[g], 1, LANES, None))
        else:
            outs.append(natural(o_refs[g], dil, d, scratch[k]))
            stats.append(natural(stat_refs[g], dil, LANES, scratch[k + 1]))
            k += 2
    heads = d // head_dim
    head_lane = lax.broadcasted_iota(jnp.int32, (tm, LANES), 1) < heads
    mx = functools.reduce(jnp.maximum, stats)
    es = [jnp.exp2(l - mx) for l in stats]
    dens = [pltpu.roll(l, LANES - heads, 1) for l in stats]
    tot = functools.reduce(lambda a, b_: a + b_, [e * dn for e, dn in zip(es, dens)])
    tot = jnp.where(head_lane, tot, 1.0)
    expand = expand_ref[...]
    mixed = jnp.zeros((tm, d), F32)
    for e, o in zip(es, outs):
        wgt = jnp.where(head_lane, e / tot, 0.0)
        hi = wgt.astype(BF16)
        lo = (wgt - hi.astype(F32)).astype(BF16)
        mixed = mixed + _dot(jnp.concatenate([hi, lo], axis=1), expand) * o
    m = _dot(mixed.astype(BF16), w_ref[...])
    x = x_ref[0] + _rmsnorm(m, g_post_ref[...])
    out_ref[0] = _ffn_tail(x, ffn_refs)


def _attn_out(os_, stats, x, w_out, g_post, dils, ffn):
    b, seq, d = x.shape
    tm = TOKEN_TILE
    in_specs = []
    for dil in dils:
        in_specs.append(pl.BlockSpec((1, tm // dil, dil * d), lambda bi, i: (bi, i, 0)))
    for dil in dils:
        in_specs.append(pl.BlockSpec((1, tm // dil, dil * LANES), lambda bi, i: (bi, i, 0)))
    expand = (lax.broadcasted_iota(jnp.int32, (2 * LANES, d), 1) // ATTN_HEAD_DIM
              == lax.broadcasted_iota(jnp.int32, (2 * LANES, d), 0) % LANES).astype(BF16)
    in_specs += [pl.BlockSpec((1, tm, d), lambda bi, i: (bi, i, 0)), _resident(w_out.shape), _resident((1, d)),
                 *ffn.specs(), _resident(expand.shape)]
    scratch = []
    for dil in dils:
        if dil > 1:
            scratch += [pltpu.VMEM((d // LANES, tm, LANES), F32), pltpu.VMEM((1, tm, LANES), F32)]
    return pl.pallas_call(
        functools.partial(_attn_out_kernel, dils=dils, head_dim=ATTN_HEAD_DIM),
        grid=(b, seq // tm),
        in_specs=in_specs,
        out_specs=pl.BlockSpec((1, tm, d), lambda bi, i: (bi, i, 0)),
        out_shape=jax.ShapeDtypeStruct((b, seq, d), F32),
        scratch_shapes=scratch,
        compiler_params=_params("parallel", "parallel"),
        name="attn_out_ffn",
    )(*os_, *stats, x, w_out, g_post, *ffn, expand)


def _rotary_tables(seq):
    inv = ROPE_THETA ** (-jnp.arange(0, ROT_DIM, 2, dtype=F32) / ROT_DIM)
    ang = jnp.arange(seq, dtype=F32)[:, None] * inv[None, :]
    cos, sin = jnp.cos(ang), jnp.sin(ang)
    half = ROT_DIM // 2
    rest = ATTN_HEAD_DIM - ROT_DIM
    ones = jnp.ones((seq, rest), F32)
    zeros_h = jnp.zeros((seq, half), F32)
    zeros_r = jnp.zeros((seq, rest), F32)
    per_head = (
        jnp.concatenate([cos, cos, ones], axis=1),
        jnp.concatenate([-sin, zeros_h, zeros_r], axis=1),
        jnp.concatenate([zeros_h, sin, zeros_r], axis=1),
    )
    return [jnp.tile(tbl, (1, LANES // ATTN_HEAD_DIM)) for tbl in per_head]


def _attention_layer(x, g_pre, g_post, w_in, w_out, ffn):
    b, seq, d = x.shape
    x2d = x.reshape(b * seq, d)
    dils = tuple(dil for _, dil in DILATION_GROUPS)
    os_, stats = [], []
    qkvs = _attn_proj(x2d, g_pre, w_in, _rotary_tables(seq), dils, seq)
    for qkv, (window, dil) in zip(qkvs, DILATION_GROUPS):
        o, stat = _band_attn(qkv, dil, window // (2 * dil))
        os_.append(o)
        stats.append(stat)
    return _attn_out(os_, stats, x, w_out, g_post, dils, ffn)


def _scaled_bf16(w, lo, hi, scale):
    col = jnp.arange(w.shape[-1])
    return (w * jnp.where((col >= lo) & (col < hi), scale, 1.0).astype(w.dtype)).astype(BF16)


def kernel(x, norm_w, ffn_w_in, ffn_w_out, ret_w_in, ret_log1m_decay, ret_w_out, conv_w_in, conv_b_in, conv_w_dw,
           conv_b_dw, conv_ln_g, conv_ln_b, conv_w_out, conv_b_out, attn_w_in, attn_w_out):
    depth = norm_w.shape[0]
    n_mixers = 3
    qk_w = RET_HEADS * RET_QK_DIM
    attn_gw = 3 * ATTN_HEADS * ATTN_HEAD_DIM
    attn_col = jnp.arange(attn_w_in.shape[-1]) % attn_gw
    attn_scale = jnp.where(attn_col < ATTN_HEADS * ATTN_HEAD_DIM, ATTN_HEAD_DIM ** -0.5 * LOG2_E, 1.0).astype(F32)
    row = lambda v: v.reshape(1, -1)
    for i in range(depth):
        kind, j = i % n_mixers, i // n_mixers
        g_pre, g_post = row(norm_w[i, 0]), row(norm_w[i, 1])
        ffn = Ffn(row(norm_w[i, 2]), ffn_w_in[i].astype(BF16), ffn_w_out[i].astype(BF16), row(norm_w[i, 3]))
        if kind == 0:
            w_in = _scaled_bf16(ret_w_in[j], qk_w, 2 * qk_w, RET_QK_DIM ** -0.5)
            x = _retention_layer(x, g_pre, g_post, w_in, ret_log1m_decay[j], ret_w_out[j].astype(BF16), ffn)
        elif kind == 1:
            x = _conv_layer(x, g_pre, g_post, conv_w_in[j].astype(BF16), row(conv_b_in[j]), conv_w_dw[j],
                            row(conv_b_dw[j]), row(conv_ln_g[j]), row(conv_ln_b[j]), conv_w_out[j].astype(BF16),
                            row(conv_b_out[j]), ffn)
        else:
            w_in = (attn_w_in[j] * attn_scale).astype(BF16)
            x = _attention_layer(x, g_pre, g_post, w_in, attn_w_out[j].astype(BF16), ffn)
    return x
```

```python
import functools
from typing import NamedTuple

import jax
import jax.numpy as jnp
from jax import lax
from jax.experimental import pallas as pl
from jax.experimental.pallas import tpu as pltpu

F32 = jnp.float32
BF16 = jnp.bfloat16

RMS_EPS = 1e-6
LN_EPS = 1e-5
MASK_VALUE = -1e30
LOG2_E = 1.4426950408889634

RET_HEADS = 4
RET_QK_DIM = 256
RET_V_DIM = 512
RET_ROPE_BASE = 10000.0
ATTN_HEADS = 16
ATTN_HEAD_DIM = 64
DILATION_GROUPS = ((128, 1), (512, 4), (2048, 16))
ROPE_THETA = 500000.0
ROT_DIM = ATTN_HEAD_DIM // 4

LANES = 128
SUBLANES = 8
V7X_VMEM_LIMIT_BYTES = 56 * 1024 * 1024

TOKEN_TILE = 512
PROJ_TILE = 1024
RET_CHUNK = 256
ATTN_Q_BLOCK = 128
ATTN_Q_STEP = 1024
CONV_HALO = 16
CONV_STRIP = 128
PROJ_COLS = 512


def _params(*semantics):
    return pltpu.CompilerParams(dimension_semantics=semantics, vmem_limit_bytes=V7X_VMEM_LIMIT_BYTES)


def _resident(shape):
    zeros = (0,) * len(shape)
    return pl.BlockSpec(shape, lambda *_: zeros, pipeline_mode=pl.Buffered(1))


def _rmsnorm(x, g):
    return x * lax.rsqrt(jnp.mean(x * x, axis=-1, keepdims=True) + RMS_EPS) * g


def _silu(x):
    return x * jax.nn.sigmoid(x)


def _dot(a, b):
    return jnp.dot(a, b, preferred_element_type=F32)


def _dot_nt(a, b):
    return lax.dot_general(a, b, (((1,), (1,)), ((), ())), preferred_element_type=F32)


def _dot_tn(a, b):
    return lax.dot_general(a, b, (((0,), (0,)), ((), ())), preferred_element_type=F32)


def _ffn_hidden_chunk(hidden):
    for cand in (512, 256, 128):
        if hidden % cand == 0:
            return cand
    raise ValueError(f"hidden width {hidden} is not a multiple of {LANES}")


class Ffn(NamedTuple):
    g_pre: jax.Array
    w_in: jax.Array
    w_out: jax.Array
    g_post: jax.Array

    def specs(self):
        return [_resident(a.shape) for a in self]


def _ffn_tail(x, ffn_refs):
    g_pre_ref, w_in_ref, w_out_ref, g_post_ref = ffn_refs
    hidden = w_out_ref.shape[0]
    chunk = _ffn_hidden_chunk(hidden)
    xn = _rmsnorm(x, g_pre_ref[...]).astype(BF16)
    acc = jnp.zeros(x.shape, F32)
    for lo in range(0, hidden, chunk):
        a = _dot(xn, w_in_ref[:, lo:lo + chunk])
        b = _dot(xn, w_in_ref[:, hidden + lo:hidden + lo + chunk])
        h = (_silu(a) * b).astype(BF16)
        acc = acc + _dot(h, w_out_ref[lo:lo + chunk, :])
    return x + _rmsnorm(acc, g_post_ref[...])


def _ret_proj_kernel(x_ref, g_ref, w_ref, cos_ref, sin_ref, o_ref, *, rot_width, head_dim, plain_chunk):
    xn = _rmsnorm(x_ref[...], g_ref[...]).astype(BF16)
    cos = cos_ref[...]
    sin = sin_ref[...]
    half = head_dim // 2
    for h in range(rot_width // head_dim):
        lo = h * head_dim
        a = _dot(xn, w_ref[:, lo:lo + head_dim])
        x1 = a[:, :half]
        x2 = a[:, half:]
        o_ref[:, lo:lo + half] = (x1 * cos - x2 * sin).astype(BF16)
        o_ref[:, lo + half:lo + head_dim] = (x2 * cos + x1 * sin).astype(BF16)
    width = w_ref.shape[1]
    for lo in range(rot_width, width, plain_chunk):
        o_ref[:, lo:lo + plain_chunk] = _dot(xn, w_ref[:, lo:lo + plain_chunk]).astype(BF16)


def _ret_proj(x2d, g, w, cos, sin, seq):
    t, d = x2d.shape
    width = w.shape[1]
    tm = PROJ_TILE
    blocks_per_seq = seq // tm
    half = RET_QK_DIM // 2
    return pl.pallas_call(
        functools.partial(_ret_proj_kernel, rot_width=2 * RET_HEADS * RET_QK_DIM, head_dim=RET_QK_DIM,
                          plain_chunk=PROJ_COLS),
        grid=(t // tm,),
        in_specs=[
            pl.BlockSpec((tm, d), lambda i: (i, 0)),
            _resident((1, d)),
            _resident(w.shape),
            pl.BlockSpec((tm, half), lambda i: (i % blocks_per_seq, 0)),
            pl.BlockSpec((tm, half), lambda i: (i % blocks_per_seq, 0)),
        ],
        out_specs=pl.BlockSpec((tm, width), lambda i: (i, 0)),
        out_shape=jax.ShapeDtypeStruct((t, width), BF16),
        compiler_params=_params("parallel"),
        name="ret_proj",
    )(x2d, g, w, cos, sin)


def _ret_core_kernel(decay_ref, q_ref, k_ref, v_ref, o_ref, acc_ref, stf_ref, stb_ref, *, chunk):
    head = pl.program_id(1)
    seq = q_ref.shape[1]
    n_chunks = seq // chunk
    c = chunk

    def log_gamma(direction):
        log1m = jnp.full((c, 1), decay_ref[direction, head], F32)
        return jnp.log(1.0 - jnp.exp(log1m))

    lgf = log_gamma(0)
    lgb = log_gamma(1)
    idx = lax.broadcasted_iota(jnp.int32, (c, 1), 0).astype(F32)
    xi_f = jnp.exp(lgf * (idx + 1.0))
    zeta_f = jnp.exp(lgf * (c - 1.0 - idx))
    decay_f = jnp.exp(lgf * float(c))
    xi_b = jnp.exp(lgb * (c - idx))
    zeta_b = jnp.exp(lgb * idx)
    decay_b = jnp.exp(lgb * float(c))
    diff = (lax.broadcasted_iota(jnp.int32, (c, c), 0) - lax.broadcasted_iota(jnp.int32, (c, c), 1)).astype(F32)
    dmat = jnp.exp(jnp.where(diff >= 0.0, lgf, -lgb) * diff)

    def rows(i):
        return pl.ds(pl.multiple_of(i * c, c), c)

    def state_update(st, decay, kc, zeta, vc):
        kz = (kc.astype(F32) * zeta).astype(BF16)
        return st * decay + _dot_tn(kz, vc)

    stf_ref[...] = jnp.zeros(stf_ref.shape, F32)
    stb_ref[...] = jnp.zeros(stb_ref.shape, F32)

    def parts(a, b):
        ra, rb = rows(a), rows(b)
        qa, ka, va = q_ref[0, ra, :], k_ref[0, ra, :], v_ref[0, ra, :]
        qb, kb, vb = q_ref[0, rb, :], k_ref[0, rb, :], v_ref[0, rb, :]
        stf = stf_ref[...]
        stb = stb_ref[...]
        scores = _dot_nt(qa, ka)
        cross_f = _dot(qa, stf.astype(BF16))
        cross_b = _dot(qb, stb.astype(BF16))
        stf_ref[...] = state_update(stf, decay_f, ka, zeta_f, va)
        stb_ref[...] = state_update(stb, decay_b, kb, zeta_b, vb)
        intra = _dot((scores * dmat).astype(BF16), va)
        return intra + xi_f * cross_f, xi_b * cross_b

    def finish(i, o):
        o = o * lax.rsqrt(jnp.mean(o * o, axis=-1, keepdims=True) + RMS_EPS)
        o_ref[0, rows(i), :] = o.astype(o_ref.dtype)

    half = n_chunks // 2

    def approach(j, carry):
        a, b = j, n_chunks - 1 - j
        part_a, part_b = parts(a, b)
        acc_ref[rows(a), :] = part_a
        acc_ref[rows(b), :] = part_b
        return carry

    def cross(j, carry):
        a, b = half + j, half - 1 - j
        part_a, part_b = parts(a, b)
        finish(a, acc_ref[rows(a), :] + part_a)
        finish(b, acc_ref[rows(b), :] + part_b)
        return carry

    lax.fori_loop(0, half, approach, 0, unroll=True)
    lax.fori_loop(0, half, cross, 0, unroll=True)


def _ret_core(proj3d, log1m_decay):
    b, seq, _ = proj3d.shape
    dk, dv, heads = RET_QK_DIM, RET_V_DIM, RET_HEADS
    assert seq % (2 * RET_CHUNK) == 0
    k_blk0 = heads
    v_blk0 = 2 * heads * dk // dv
    return pl.pallas_call(
        functools.partial(_ret_core_kernel, chunk=RET_CHUNK),
        grid=(b, heads),
        in_specs=[
            pl.BlockSpec(memory_space=pltpu.SMEM),
            pl.BlockSpec((1, seq, dk), lambda bi, h: (bi, 0, h)),
            pl.BlockSpec((1, seq, dk), lambda bi, h: (bi, 0, k_blk0 + h)),
            pl.BlockSpec((1, seq, dv), lambda bi, h: (bi, 0, v_blk0 + h)),
        ],
        out_specs=pl.BlockSpec((1, seq, dv), lambda bi, h: (bi, 0, h)),
        out_shape=jax.ShapeDtypeStruct((b, seq, heads * dv), BF16),
        scratch_shapes=[pltpu.VMEM((seq, dv), F32), pltpu.VMEM((dk, dv), F32), pltpu.VMEM((dk, dv), F32)],
        compiler_params=_params("parallel", "parallel"),
        name="ret_core",
    )(log1m_decay, proj3d, proj3d, proj3d)


def _ret_out_kernel(o_ref, gate_ref, x_ref, w_ref, g_post_ref, *rest):
    *ffn_refs, out_ref = rest
    y = (_silu(gate_ref[...].astype(F32)) * o_ref[...].astype(F32)).astype(BF16)
    x = x_ref[...] + _rmsnorm(_dot(y, w_ref[...]), g_post_ref[...])
    out_ref[...] = _ffn_tail(x, ffn_refs)


def _ret_out(o2d, proj2d, x2d, w_out, g_post, ffn):
    t, d = x2d.shape
    vw = o2d.shape[1]
    gate_blk = proj2d.shape[1] // vw - 1
    tm = TOKEN_TILE
    return pl.pallas_call(
        _ret_out_kernel,
        grid=(t // tm,),
        in_specs=[
            pl.BlockSpec((tm, vw), lambda i: (i, 0)),
            pl.BlockSpec((tm, vw), lambda i: (i, gate_blk)),
            pl.BlockSpec((tm, d), lambda i: (i, 0)),
            _resident(w_out.shape),
            _resident((1, d)),
            *ffn.specs(),
        ],
        out_specs=pl.BlockSpec((tm, d), lambda i: (i, 0)),
        out_shape=jax.ShapeDtypeStruct((t, d), F32),
        compiler_params=_params("parallel"),
        name="ret_out_ffn",
    )(o2d, proj2d, x2d, w_out, g_post, *ffn)


def _retention_layer(x, g_pre, g_post, w_in, log1m_decay, w_out, ffn):
    b, seq, d = x.shape
    inv = 1.0 / (RET_ROPE_BASE ** jnp.linspace(0.0, 1.0, RET_QK_DIM // 2, dtype=F32))
    ang = jnp.arange(seq, dtype=F32)[:, None] * inv[None, :]
    x2d = x.reshape(b * seq, d)
    proj = _ret_proj(x2d, g_pre, w_in, jnp.cos(ang), jnp.sin(ang), seq)
    o = _ret_core(proj.reshape(b, seq, -1), log1m_decay)
    return _ret_out(o.reshape(b * seq, -1), proj, x2d, w_out, g_post, ffn).reshape(b, seq, d)


def _conv_in_kernel(x_ref, g_ref, w_ref, b_ref, u_ref, *, chunk):
    d = u_ref.shape[1]
    xn = _rmsnorm(x_ref[...], g_ref[...]).astype(BF16)
    for lo in range(0, d, chunk):
        a = _dot(xn, w_ref[:, lo:lo + chunk]) + b_ref[:, lo:lo + chunk]
        gate = _dot(xn, w_ref[:, d + lo:d + lo + chunk]) + b_ref[:, d + lo:d + lo + chunk]
        u_ref[:, lo:lo + chunk] = (a * jax.nn.sigmoid(gate)).astype(BF16)


def _conv_in(x2d, g, w_in, b_in):
    t, d = x2d.shape
    tm = PROJ_TILE
    return pl.pallas_call(
        functools.partial(_conv_in_kernel, chunk=PROJ_COLS),
        grid=(t // tm,),
        in_specs=[
            pl.BlockSpec((tm, d), lambda i: (i, 0)),
            _resident((1, d)),
            _resident(w_in.shape),
            _resident(b_in.shape),
        ],
        out_specs=pl.BlockSpec((tm, d), lambda i: (i, 0)),
        out_shape=jax.ShapeDtypeStruct((t, d), BF16),
        compiler_params=_params("parallel"),
        name="conv_in",
    )(x2d, g, w_in, b_in)


def _conv_out_kernel(u_prev_ref, u_ref, u_next_ref, x_ref, w_dw_ref, b_dw_ref, ln_g_ref, ln_b_ref,
                     w_ref, b_out_ref, g_post_ref, g2_ref, f_in_ref, f_out_ref, g3_ref, shift_ref, out_ref, ext_ref, *,
                     strip):
    i = pl.program_id(1)
    tm = u_ref.shape[1]
    halo = u_prev_ref.shape[1]
    taps = w_dw_ref.shape[0]
    pad = (taps - 1) // 2
    zero = jnp.zeros((halo, u_ref.shape[2]), BF16)
    ext_ref[0:halo, :] = jnp.where(i > 0, u_prev_ref[0], zero)
    ext_ref[halo:halo + tm, :] = u_ref[0]
    ext_ref[halo + tm:, :] = jnp.where(i < pl.num_programs(1) - 1, u_next_ref[0], zero)
    rows_w = strip + 2 * halo
    strips = []
    for r0 in range(0, tm, strip):
        window = ext_ref[r0:r0 + rows_w, :]
        moved = _dot(shift_ref[...], window)
        aligned = window.astype(F32)
        acc = jnp.zeros((strip, u_ref.shape[2]), F32) + b_dw_ref[...]
        for res in range(SUBLANES):
            for k in range(taps):
                off = halo - pad + k
                if off % SUBLANES == res:
                    if res == 0:
                        src = aligned[off:off + strip, :]
                    else:
                        lo = (res - 1) * rows_w + off - res
                        src = moved[lo:lo + strip, :]
                    acc = acc + src * w_dw_ref[k:k + 1, :]
        strips.append(acc)
    acc = jnp.concatenate(strips, axis=0)
    mu = jnp.mean(acc, axis=-1, keepdims=True)
    cen = acc - mu
    var = jnp.mean(cen * cen, axis=-1, keepdims=True)
    y = cen * lax.rsqrt(var + LN_EPS) * ln_g_ref[...] + ln_b_ref[...]
    m = _dot(_silu(y).astype(BF16), w_ref[...]) + b_out_ref[...]
    x = x_ref[0] + _rmsnorm(m, g_post_ref[...])
    out_ref[0] = _ffn_tail(x, (g2_ref, f_in_ref, f_out_ref, g3_ref))


def _conv_out(u, x, w_dw, b_dw, ln_g, ln_b, w_out, b_out, g_post, ffn):
    b, seq, d = x.shape
    tm = TOKEN_TILE
    halo = CONV_HALO
    r = tm // halo
    last = seq // halo - 1
    assert halo >= (w_dw.shape[0] - 1) // 2 and tm % CONV_STRIP == 0
    rows_w = CONV_STRIP + 2 * halo
    out_row = lax.broadcasted_iota(jnp.int32, ((SUBLANES - 1) * rows_w, rows_w), 0)
    in_row = lax.broadcasted_iota(jnp.int32, ((SUBLANES - 1) * rows_w, rows_w), 1)
    shift = (in_row == out_row % rows_w + out_row // rows_w + 1).astype(BF16)
    return pl.pallas_call(
        functools.partial(_conv_out_kernel, strip=CONV_STRIP),
        grid=(b, seq // tm),
        in_specs=[
            pl.BlockSpec((1, halo, d), lambda bi, i: (bi, jnp.maximum(i * r - 1, 0), 0)),
            pl.BlockSpec((1, tm, d), lambda bi, i: (bi, i, 0)),
            pl.BlockSpec((1, halo, d), lambda bi, i: (bi, jnp.minimum((i + 1) * r, last), 0)),
            pl.BlockSpec((1, tm, d), lambda bi, i: (bi, i, 0)),
            _resident(w_dw.shape),
            _resident((1, d)),
            _resident((1, d)),
            _resident((1, d)),
            _resident(w_out.shape),
            _resident((1, d)),
            _resident((1, d)),
            *ffn.specs(),
            _resident(shift.shape),
        ],
        out_specs=pl.BlockSpec((1, tm, d), lambda bi, i: (bi, i, 0)),
        out_shape=jax.ShapeDtypeStruct((b, seq, d), F32),
        scratch_shapes=[pltpu.VMEM((tm + 2 * halo, d), BF16)],
        compiler_params=_params("parallel", "parallel"),
        name="conv_out_ffn",
    )(u, u, u, x, w_dw, b_dw, ln_g, ln_b, w_out, b_out, g_post, *ffn, shift)


def _conv_layer(x, g_pre, g_post, w_in, b_in, w_dw, b_dw, ln_g, ln_b, w_out, b_out, ffn):
    b, seq, d = x.shape
    u = _conv_in(x.reshape(b * seq, d), g_pre, w_in, b_in).reshape(b, seq, d)
    return _conv_out(u, x, w_dw, b_dw, ln_g, ln_b, w_out, b_out, g_post, ffn)


def _attn_proj_kernel(x_ref, g_ref, w_ref, c_ref, s_lo_ref, s_hi_ref, *rest, dils, rot_width, chunk):
    o_refs = rest[:len(dils)]
    xs_ref = rest[len(dils)]
    xp_refs = rest[len(dils) + 1:]
    tm = x_ref.shape[0]
    group_width = w_ref.shape[1] // len(dils)
    half = ROT_DIM // 2
    xn = _rmsnorm(x_ref[...], g_ref[...])
    for cb in range(xs_ref.shape[0]):
        xs_ref[cb] = xn[:, cb * LANES:(cb + 1) * LANES]
    staged = 0
    for g, dil in enumerate(dils):
        n = tm // dil
        if dil == 1:
            xp = xn.astype(BF16)
        else:
            xp_ref = xp_refs[staged]
            staged += 1
            for r in range(dil):
                for cb in range(xs_ref.shape[0]):
                    xp_ref[r * n:(r + 1) * n, cb * LANES:(cb + 1) * LANES] = (
                        xs_ref[cb, pl.ds(r, n, stride=dil), :].astype(BF16))
            xp = xp_ref[...]

        def permuted(tbl_ref):
            if dil == 1:
                return tbl_ref[...]
            return jnp.concatenate([tbl_ref[pl.ds(r, n, stride=dil), :] for r in range(dil)], axis=0)

        cmul = permuted(c_ref)
        s_lo = permuted(s_lo_ref)
        s_hi = permuted(s_hi_ref)
        for lo in range(0, group_width, chunk):
            a = _dot(xp, w_ref[:, g * group_width + lo:g * group_width + lo + chunk])
            for sub in range(0, chunk, LANES):
                piece = a[:, sub:sub + LANES]
                if lo < rot_width:
                    piece = (piece * cmul + pltpu.roll(piece, LANES - half, 1) * s_lo
                             + pltpu.roll(piece, half, 1) * s_hi)
                piece = piece.astype(BF16)
                for r in range(dil):
                    o_refs[g][0, r, :, lo + sub:lo + sub + LANES] = piece[r * n:(r + 1) * n, :]


def _attn_proj(x2d, g, w, tables, dils, seq):
    t, d = x2d.shape
    width = w.shape[1] // len(dils)
    tm = TOKEN_TILE
    blocks_per_seq = seq // tm
    b = t // seq
    table_spec = pl.BlockSpec((tm, LANES), lambda i: (i % blocks_per_seq, 0))
    return pl.pallas_call(
        functools.partial(_attn_proj_kernel, dils=dils, rot_width=2 * ATTN_HEADS * ATTN_HEAD_DIM, chunk=PROJ_COLS),
        grid=(t // tm,),
        in_specs=[
            pl.BlockSpec((tm, d), lambda i: (i, 0)),
            _resident((1, d)),
            _resident(w.shape),
            table_spec, table_spec, table_spec,
        ],
        out_specs=[pl.BlockSpec((1, dil, tm // dil, width),
                                lambda i: (i // blocks_per_seq, 0, i % blocks_per_seq, 0)) for dil in dils],
        out_shape=[jax.ShapeDtypeStruct((b, dil, seq // dil, width), BF16) for dil in dils],
        scratch_shapes=[pltpu.VMEM((d // LANES, tm, LANES), F32)]
        + [pltpu.VMEM((tm, d), BF16) for dil in dils if dil > 1],
        compiler_params=_params("parallel"),
        name="attn_proj",
    )(x2d, g, w, *tables)


def _band_attn_kernel(q_ref, kp_ref, kc_ref, kn_ref, vp_ref, vc_ref, vn_ref, o_ref, stat_ref, bias_ref, *,
                      heads, head_dim, radius, sub_len, tq):
    i = pl.program_id(1)
    rows_step = q_ref.shape[1]
    n_sub = rows_step // tq
    nk = tq + 2 * radius
    row = lax.broadcasted_iota(jnp.int32, (tq, nk), 0)
    col = lax.broadcasted_iota(jnp.int32, (tq, nk), 1)
    off = col - row
    band = (off >= 0) & (off <= 2 * radius)
    lane = lax.broadcasted_iota(jnp.int32, (tq, LANES), 1)
    first = lane < head_dim
    keep_first = first.astype(F32).astype(BF16)
    keep_second = (1.0 - first.astype(F32)).astype(BF16)
    ones = jnp.ones((nk, LANES), BF16)

    width = q_ref.shape[2]

    def window(prev_ref, cur_ref, next_ref, sq, r0):
        before = prev_ref[sq] if r0 == 0 else cur_ref[sq, r0 - radius:r0, :]
        after = next_ref[sq] if r0 + tq == rows_step else cur_ref[sq, r0 + tq:r0 + tq + radius, :]
        return jnp.concatenate([before, cur_ref[sq, r0:r0 + tq, :], after], axis=0)

    for sub in range(n_sub):
        r0 = sub * tq
        kpos = i * rows_step + r0 - radius + col
        bias_ref[sub] = jnp.where(band & (kpos >= 0) & (kpos < sub_len), 0.0, MASK_VALUE)
        for sq in range(q_ref.shape[0]):
            kwin = window(kp_ref, kc_ref, kn_ref, sq, r0)
            vwin = window(vp_ref, vc_ref, vn_ref, sq, r0)
            stat = jnp.zeros((tq, LANES), F32)
            for pair in range(heads // 2):
                cols = slice(pair * LANES, (pair + 1) * LANES)
                qp = q_ref[sq, r0:r0 + tq, cols]
                s2 = _dot_nt(jnp.concatenate([qp * keep_first, qp * keep_second], axis=0), kwin[:, cols])
                outs = []
                for k in range(2):
                    h = 2 * pair + k
                    s = s2[k * tq:(k + 1) * tq] + bias_ref[sub]
                    mx = jnp.max(s, axis=-1, keepdims=True)
                    p = jnp.exp2(s - mx).astype(BF16)
                    pv = _dot(p, jnp.concatenate([vwin[:, cols], ones], axis=1))
                    outs.append(pv[:, :LANES])
                    stat = jnp.where(lane == h, mx, stat)
                    stat = jnp.where(lane == heads + h, pv[:, LANES:], stat)
                o_lo = sq * width + pair * LANES
                o_ref[0, r0:r0 + tq, o_lo:o_lo + LANES] = jnp.where(first, outs[0], outs[1]).astype(o_ref.dtype)
            stat_ref[0, r0:r0 + tq, sq * LANES:(sq + 1) * LANES] = stat


def _band_attn(qkv, dil, radius):
    assert 2 * ATTN_HEADS <= LANES and 2 * ATTN_HEAD_DIM == LANES
    b, _, sub_len, w3 = qkv.shape
    width = w3 // 3
    tq = ATTN_Q_BLOCK
    rows_step = min(ATTN_Q_STEP, sub_len)
    nstep = sub_len // rows_step
    assert radius <= tq and rows_step % tq == 0 and sub_len % rows_step == 0 and rows_step % radius == 0
    flat = qkv.reshape(b * dil, sub_len, w3)
    per_step = rows_step // radius
    last_halo = sub_len // radius - 1
    n_seq = max(1, min(dil, ATTN_Q_STEP // rows_step))
    assert dil % n_seq == 0
    per_row = dil // n_seq

    def body(col):
        return pl.BlockSpec((n_seq, rows_step, width), lambda s, i: (s, i, col))

    def before(col):
        return pl.BlockSpec((n_seq, radius, width), lambda s, i: (s, jnp.maximum(i * per_step - 1, 0), col))

    def after(col):
        return pl.BlockSpec((n_seq, radius, width), lambda s, i: (s, jnp.minimum((i + 1) * per_step, last_halo), col))

    return pl.pallas_call(
        functools.partial(_band_attn_kernel, heads=ATTN_HEADS, head_dim=ATTN_HEAD_DIM, radius=radius,
                          sub_len=sub_len, tq=tq),
        grid=(b * per_row, nstep),
        in_specs=[body(0), before(1), body(1), after(1), before(2), body(2), after(2)],
        out_specs=[
            pl.BlockSpec((1, rows_step, n_seq * width), lambda s, i: (s // per_row, i, s % per_row)),
            pl.BlockSpec((1, rows_step, n_seq * LANES), lambda s, i: (s // per_row, i, s % per_row)),
        ],
        out_shape=[
            jax.ShapeDtypeStruct((b, sub_len, dil * width), BF16),
            jax.ShapeDtypeStruct((b, sub_len, dil * LANES), F32),
        ],
        scratch_shapes=[pltpu.VMEM((rows_step // tq, tq, tq + 2 * radius), F32)],
        compiler_params=_params("parallel", "parallel"),
        name=f"band_attn_d{dil}",
    )(flat, flat, flat, flat, flat, flat, flat)


def _attn_out_kernel(*refs, dils, head_dim):
    n_groups = len(dils)
    o_refs = refs[:n_groups]
    stat_refs = refs[n_groups:2 * n_groups]
    x_ref, w_ref, g_post_ref, *ffn_refs, expand_ref, out_ref = refs[2 * n_groups:2 * n_groups + 9]
    scratch = refs[2 * n_groups + 9:]
    tm, d = x_ref.shape[1], x_ref.shape[2]

    def natural(ref, dil, width, buf):
        if dil == 1:
            return ref[0].astype(F32)
        n = tm // dil
        for r in range(dil):
            for cb in range(width // LANES):
                lo = r * width + cb * LANES
                buf[cb, pl.ds(r, n, stride=dil), :] = ref[0, :, lo:lo + LANES].astype(F32)
        return jnp.concatenate([buf[cb] for cb in range(width // LANES)], axis=1)

    outs, stats = [], []
    k = 0
    for g, dil in enumerate(dils):
        if dil == 1:
            outs.append(natural(o_refs[g], 1, d, None))
            stats.append(natural(stat_refs[g], 1, LANES, None))
        else:
            outs.append(natural(o_refs[g], dil, d, scratch[k]))
            stats.append(natural(stat_refs[g], dil, LANES, scratch[k + 1]))
            k += 2
    heads = d // head_dim
    head_lane = lax.broadcasted_iota(jnp.int32, (tm, LANES), 1) < heads
    mx = functools.reduce(jnp.maximum, stats)
    es = [jnp.exp2(l - mx) for l in stats]
    dens = [pltpu.roll(l, LANES - heads, 1) for l in stats]
    tot = functools.reduce(lambda a, b_: a + b_, [e * dn for e, dn in zip(es, dens)])
    tot = jnp.where(head_lane, tot, 1.0)
    expand = expand_ref[...]
    mixed = jnp.zeros((tm, d), F32)
    for e, o in zip(es, outs):
        wgt = jnp.where(head_lane, e / tot, 0.0)
        hi = wgt.astype(BF16)
        lo = (wgt - hi.astype(F32)).astype(BF16)
        mixed = mixed + _dot(jnp.concatenate([hi, lo], axis=1), expand) * o
    m = _dot(mixed.astype(BF16), w_ref[...])
    x = x_ref[0] + _rmsnorm(m, g_post_ref[...])
    out_ref[0] = _ffn_tail(x, ffn_refs)


def _attn_out(os_, stats, x, w_out, g_post, dils, ffn):
    b, seq, d = x.shape
    tm = TOKEN_TILE
    in_specs = []
    for dil in dils:
        in_specs.append(pl.BlockSpec((1, tm // dil, dil * d), lambda bi, i: (bi, i, 0)))
    for dil in dils:
        in_specs.append(pl.BlockSpec((1, tm // dil, dil * LANES), lambda bi, i: (bi, i, 0)))
    expand = (lax.broadcasted_iota(jnp.int32, (2 * LANES, d), 1) // ATTN_HEAD_DIM
              == lax.broadcasted_iota(jnp.int32, (2 * LANES, d), 0) % LANES).astype(BF16)
    in_specs += [pl.BlockSpec((1, tm, d), lambda bi, i: (bi, i, 0)), _resident(w_out.shape), _resident((1, d)),
                 *ffn.specs(), _resident(expand.shape)]
    scratch = []
    for dil in dils:
        if dil > 1:
            scratch += [pltpu.VMEM((d // LANES, tm, LANES), F32), pltpu.VMEM((1, tm, LANES), F32)]
    return pl.pallas_call(
        functools.partial(_attn_out_kernel, dils=dils, head_dim=ATTN_HEAD_DIM),
        grid=(b, seq // tm),
        in_specs=in_specs,
        out_specs=pl.BlockSpec((1, tm, d), lambda bi, i: (bi, i, 0)),
        out_shape=jax.ShapeDtypeStruct((b, seq, d), F32),
        scratch_shapes=scratch,
        compiler_params=_params("parallel", "parallel"),
        name="attn_out_ffn",
    )(*os_, *stats, x, w_out, g_post, *ffn, expand)


def _rotary_tables(seq):
    inv = ROPE_THETA ** (-jnp.arange(0, ROT_DIM, 2, dtype=F32) / ROT_DIM)
    ang = jnp.arange(seq, dtype=F32)[:, None] * inv[None, :]
    cos, sin = jnp.cos(ang), jnp.sin(ang)
    half = ROT_DIM // 2
    rest = ATTN_HEAD_DIM - ROT_DIM
    ones = jnp.ones((seq, rest), F32)
    zeros_h = jnp.zeros((seq, half), F32)
    zeros_r = jnp.zeros((seq, rest), F32)
    per_head = (
        jnp.concatenate([cos, cos, ones], axis=1),
        jnp.concatenate([-sin, zeros_h, zeros_r], axis=1),
        jnp.concatenate([zeros_h, sin, zeros_r], axis=1),
    )
    return [jnp.tile(tbl, (1, LANES // ATTN_HEAD_DIM)) for tbl in per_head]


def _attention_layer(x, g_pre, g_post, w_in, w_out, ffn):
    b, seq, d = x.shape
    x2d = x.reshape(b * seq, d)
    dils = tuple(dil for _, dil in DILATION_GROUPS)
    os_, stats = [], []
    qkvs = _attn_proj(x2d, g_pre, w_in, _rotary_tables(seq), dils, seq)
    for qkv, (window, dil) in zip(qkvs, DILATION_GROUPS):
        o, stat = _band_attn(qkv, dil, window // (2 * dil))
        os_.append(o)
        stats.append(stat)
    return _attn_out(os_, stats, x, w_out, g_post, dils, ffn)


def _scaled_bf16(w, lo, hi, scale):
    col = jnp.arange(w.shape[-1])
    return (w * jnp.where((col >= lo) & (col < hi), scale, 1.0).astype(w.dtype)).astype(BF16)


def kernel(x, norm_w, ffn_w_in, ffn_w_out, ret_w_in, ret_log1m_decay, ret_w_out, conv_w_in, conv_b_in, conv_w_dw,
           conv_b_dw, conv_ln_g, conv_ln_b, conv_w_out, conv_b_out, attn_w_in, attn_w_out):
    depth = norm_w.shape[0]
    n_mixers = 3
    qk_w = RET_HEADS * RET_QK_DIM
    attn_gw = 3 * ATTN_HEADS * ATTN_HEAD_DIM
    attn_col = jnp.arange(attn_w_in.shape[-1]) % attn_gw
    attn_scale = jnp.where(attn_col < ATTN_HEADS * ATTN_HEAD_DIM, ATTN_HEAD_DIM ** -0.5 * LOG2_E, 1.0).astype(F32)
    row = lambda v: v.reshape(1, -1)
    for i in range(depth):
        kind, j = i % n_mixers, i // n_mixers
        g_pre, g_post = row(norm_w[i, 0]), row(norm_w[i, 1])
        ffn = Ffn(row(norm_w[i, 2]), ffn_w_in[i].astype(BF16), ffn_w_out[i].astype(BF16), row(norm_w[i, 3]))
        if kind == 0:
            w_in = _scaled_bf16(ret_w_in[j], qk_w, 2 * qk_w, RET_QK_DIM ** -0.5)
            x = _retention_layer(x, g_pre, g_post, w_in, ret_log1m_decay[j], ret_w_out[j].astype(BF16), ffn)
        elif kind == 1:
            x = _conv_layer(x, g_pre, g_post, conv_w_in[j].astype(BF16), row(conv_b_in[j]), conv_w_dw[j],
                            row(conv_b_dw[j]), row(conv_ln_g[j]), row(conv_ln_b[j]), conv_w_out[j].astype(BF16),
                            row(conv_b_out[j]), ffn)
        else:
            w_in = (attn_w_in[j] * attn_scale).astype(BF16)
            x = _attention_layer(x, g_pre, g_post, w_in, attn_w_out[j].astype(BF16), ffn)
    return x
```
